```python
import jax, jax.numpy as jnp
from jax import lax
import numpy as np

D_MODEL = 4096
BATCH = 4
SEQ = 2048
DEPTH = 4
DEC_BATCH = 128
DEC_SEQ = 1
PAST_LEN = 16384
PAGE_SIZE = 128

N_MIXERS = 2
N_DELTA = (DEPTH + 1) // 2
N_CONF = DEPTH // 2
GDN_QK_HEADS = 16
GDN_V_HEADS = 32
GDN_HEAD_K = 128
GDN_HEAD_V = 128
GDN_KEY_DIM = GDN_QK_HEADS * GDN_HEAD_K
GDN_VAL_DIM = GDN_V_HEADS * GDN_HEAD_V
GDN_CONV_DIM = 2 * GDN_KEY_DIM + GDN_VAL_DIM
GDN_CONV_W = 4
GDN_IN_DIM = GDN_CONV_DIM + GDN_VAL_DIM + 2 * GDN_V_HEADS
GDN_CHUNK = 64
CONF_DIM = D_MODEL
CONF_KERNEL = 31
D_FF = 11008
EPS = 1e-6

kernel_name = "hybrid_gdn_conformer_macaron_step"


def rms_norm(x, w):
    xf = x.astype(jnp.float32)
    y = xf * lax.rsqrt(jnp.mean(xf * xf, axis=-1, keepdims=True) + EPS)
    return (y * w.astype(jnp.float32)).astype(x.dtype)


def layer_norm(x, w, b):
    xf = x.astype(jnp.float32)
    mu = jnp.mean(xf, axis=-1, keepdims=True)
    xc = xf - mu
    y = xc * lax.rsqrt(jnp.mean(xc * xc, axis=-1, keepdims=True) + EPS)
    return (y * w.astype(jnp.float32) + b.astype(jnp.float32)).astype(x.dtype)


def swiglu(x, w_gate_up, w_down):
    g, u = jnp.split(x @ w_gate_up, 2, axis=-1)
    return (jax.nn.silu(g) * u) @ w_down


def causal_depthwise_conv(x, buf, w):
    W, C = w.shape
    xc = jnp.concatenate([buf.astype(x.dtype), x], axis=1)
    y = lax.conv_general_dilated(
        xc, w.astype(x.dtype)[:, None, :], window_strides=(1,), padding='VALID',
        dimension_numbers=('NWC', 'WIO', 'NWC'), feature_group_count=C)
    return y, xc[:, -(W - 1):]


def l2norm(x):
    return x * lax.rsqrt(jnp.sum(x * x, axis=-1, keepdims=True) + EPS)


def chunk_gated_delta_rule(q, k, v, g, beta, S0):
    f32 = jnp.float32
    B, T, H, DK = k.shape
    DV = v.shape[-1]
    C = min(GDN_CHUNK, T)
    pad = (-T) % C
    q = l2norm(q.astype(f32)) * (DK ** -0.5)
    k = l2norm(k.astype(f32))
    v = v.astype(f32)

    def to_chunks(t):
        t = jnp.pad(t, [(0, 0), (0, pad)] + [(0, 0)] * (t.ndim - 2))
        t = jnp.moveaxis(t, 2, 1)
        return t.reshape((B, H, -1, C) + t.shape[3:])

    q, k, v, g, beta = (to_chunks(t) for t in (q, k, v, g.astype(f32), beta.astype(f32)))
    g = jnp.cumsum(g, axis=-1)
    tril = jnp.tril(jnp.ones((C, C), bool))
    strict = jnp.tril(jnp.ones((C, C), bool), -1)
    diff = g[..., :, None] - g[..., None, :]
    decay = jnp.where(tril, jnp.exp(jnp.where(tril, diff, 0.0)), 0.0)
    k_beta = k * beta[..., None]
    v_beta = v * beta[..., None]
    L = jnp.where(strict, jnp.einsum('bhnid,bhnjd->bhnij', k_beta, k) * decay, 0.0)
    eye = jnp.eye(C, dtype=f32)
    Tinv = lax.linalg.triangular_solve(eye + L, jnp.broadcast_to(eye, L.shape),
                                       left_side=True, lower=True, unit_diagonal=True)
    u = jnp.einsum('bhnij,bhnje->bhnie', Tinv, v_beta)
    w = jnp.einsum('bhnij,bhnjd->bhnid', Tinv, k_beta * jnp.exp(g)[..., None])
    A_intra = jnp.where(tril, jnp.einsum('bhnid,bhnjd->bhnij', q, k) * decay, 0.0)
    g_last = g[..., -1]

    xs = tuple(jnp.moveaxis(t, 2, 0) for t in (q, k, u, w, A_intra, g, g_last))

    def step(S, inp):
        q_c, k_c, u_c, w_c, a_c, g_c, gl = inp
        v_new = u_c - jnp.einsum('bhcd,bhde->bhce', w_c, S)
        o = (jnp.einsum('bhcd,bhde->bhce', q_c * jnp.exp(g_c)[..., None], S)
             + jnp.einsum('bhij,bhje->bhie', a_c, v_new))
        S = (S * jnp.exp(gl)[..., None, None]
             + jnp.einsum('bhcd,bhce->bhde', k_c * jnp.exp(gl[..., None] - g_c)[..., None], v_new))
        return S, o

    S, o = lax.scan(step, S0.astype(f32), xs)
    o = jnp.transpose(o, (1, 0, 3, 2, 4)).reshape(B, -1, H, DV)[:, :T]
    return o, S


def gated_deltanet(x, conv_buf, S0, w_in, conv_w, a_log, dt_bias, norm_w, w_out):
    B, T, _ = x.shape
    f32 = jnp.float32
    proj = x @ w_in
    s1 = GDN_CONV_DIM
    s2 = s1 + GDN_VAL_DIM
    s3 = s2 + GDN_V_HEADS
    qkv, z, b, a = proj[..., :s1], proj[..., s1:s2], proj[..., s2:s3], proj[..., s3:]
    qkv, new_buf = causal_depthwise_conv(qkv, conv_buf, conv_w)
    qkv = jax.nn.silu(qkv)
    q = qkv[..., :GDN_KEY_DIM].reshape(B, T, GDN_QK_HEADS, GDN_HEAD_K)
    k = qkv[..., GDN_KEY_DIM:2 * GDN_KEY_DIM].reshape(B, T, GDN_QK_HEADS, GDN_HEAD_K)
    v = qkv[..., 2 * GDN_KEY_DIM:].reshape(B, T, GDN_V_HEADS, GDN_HEAD_V)
    rep = GDN_V_HEADS // GDN_QK_HEADS
    q = jnp.repeat(q, rep, axis=2)
    k = jnp.repeat(k, rep, axis=2)
    beta = jax.nn.sigmoid(b.astype(f32))
    g = -jnp.exp(a_log.astype(f32)) * jax.nn.softplus(a.astype(f32) + dt_bias.astype(f32))
    o, S = chunk_gated_delta_rule(q, k, v, g, beta, S0)
    o = o * lax.rsqrt(jnp.mean(o * o, axis=-1, keepdims=True) + EPS) * norm_w.astype(f32)
    o = o * jax.nn.silu(z.astype(f32).reshape(B, T, GDN_V_HEADS, GDN_HEAD_V))
    o = o.astype(x.dtype).reshape(B, T, GDN_VAL_DIM)
    return o @ w_out, new_buf, S


def conformer_conv(x, conv_buf, w_pw1, b_pw1, w_dw, b_dw, ln_w, ln_b, w_pw2, b_pw2):
    h = jax.nn.glu(x @ w_pw1 + b_pw1, axis=-1)
    h, new_buf = causal_depthwise_conv(h, conv_buf, w_dw)
    h = jax.nn.silu(layer_norm(h + b_dw, ln_w, ln_b))
    return h @ w_pw2 + b_pw2, new_buf


def trunk(x, gdn_S, gdn_buf, conf_buf, p):
    new_S, new_gbuf, new_cbuf = [], [], []
    for i in range(DEPTH):
        x = x + 0.5 * swiglu(rms_norm(x, p['ffn_norm_pre'][i]), p['ffn_pre_w_gate_up'][i], p['ffn_pre_w_down'][i])
        h = rms_norm(x, p['mixer_norm'][i])
        j = i // N_MIXERS
        if i % N_MIXERS == 0:
            h, buf, S = gated_deltanet(h, gdn_buf[j], gdn_S[j], p['gdn_w_in'][j], p['gdn_conv_w'][j],
                                       p['gdn_a_log'][j], p['gdn_dt_bias'][j], p['gdn_norm_w'][j], p['gdn_w_out'][j])
            new_S.append(S)
            new_gbuf.append(buf)
        else:
            h, buf = conformer_conv(h, conf_buf[j], p['conf_w_pw1'][j], p['conf_b_pw1'][j], p['conf_w_dw'][j],
                                    p['conf_b_dw'][j], p['conf_ln_w'][j], p['conf_ln_b'][j],
                                    p['conf_w_pw2'][j], p['conf_b_pw2'][j])
            new_cbuf.append(buf)
        x = x + h
        x = x + 0.5 * swiglu(rms_norm(x, p['ffn_norm_post'][i]), p['ffn_post_w_gate_up'][i], p['ffn_post_w_down'][i])
    return rms_norm(x, p['final_norm']), jnp.stack(new_S), jnp.stack(new_gbuf), jnp.stack(new_cbuf)


def setup_inputs(seed: int = 0) -> dict:
    key = jax.random.key(seed)
    ks = iter(jax.random.split(key, 48))
    f32 = jnp.float32

    def nrm(shape, scale):
        return scale * jax.random.normal(next(ks), shape, f32)

    D, F = D_MODEL, D_FF
    a_init = jax.random.uniform(next(ks), (N_DELTA, GDN_V_HEADS), f32, minval=1.0, maxval=16.0)
    dt = jnp.exp(jax.random.uniform(next(ks), (N_DELTA, GDN_V_HEADS), f32,
                                    minval=float(np.log(1e-3)), maxval=float(np.log(1e-1))))
    return {
        'x_prompt': nrm((BATCH, SEQ, D), 1.0),
        'x_sample': nrm((DEC_BATCH, DEC_SEQ, D), 1.0),
        'state_gdn_recurrent': nrm((N_DELTA, DEC_BATCH, GDN_V_HEADS, GDN_HEAD_K, GDN_HEAD_V), 0.05),
        'state_gdn_conv': nrm((N_DELTA, DEC_BATCH, GDN_CONV_W - 1, GDN_CONV_DIM), 1.0),
        'state_conformer_conv': nrm((N_CONF, DEC_BATCH, CONF_KERNEL - 1, CONF_DIM), 0.5),
        'ffn_norm_pre': 1.0 + nrm((DEPTH, D), 0.02),
        'ffn_pre_w_gate_up': nrm((DEPTH, D, 2 * F), D ** -0.5),
        'ffn_pre_w_down': nrm((DEPTH, F, D), F ** -0.5),
        'mixer_norm': 1.0 + nrm((DEPTH, D), 0.02),
        'ffn_norm_post': 1.0 + nrm((DEPTH, D), 0.02),
        'ffn_post_w_gate_up': nrm((DEPTH, D, 2 * F), D ** -0.5),
        'ffn_post_w_down': nrm((DEPTH, F, D), F ** -0.5),
        'gdn_w_in': nrm((N_DELTA, D, GDN_IN_DIM), D ** -0.5),
        'gdn_conv_w': nrm((N_DELTA, GDN_CONV_W, GDN_CONV_DIM), GDN_CONV_W ** -0.5),
        'gdn_a_log': jnp.log(a_init),
        'gdn_dt_bias': dt + jnp.log(-jnp.expm1(-dt)),
        'gdn_norm_w': 1.0 + nrm((N_DELTA, GDN_HEAD_V), 0.02),
        'gdn_w_out': nrm((N_DELTA, GDN_VAL_DIM, D), GDN_VAL_DIM ** -0.5),
        'conf_w_pw1': nrm((N_CONF, D, 2 * CONF_DIM), D ** -0.5),
        'conf_b_pw1': nrm((N_CONF, 2 * CONF_DIM), 0.02),
        'conf_w_dw': nrm((N_CONF, CONF_KERNEL, CONF_DIM), CONF_KERNEL ** -0.5),
        'conf_b_dw': nrm((N_CONF, CONF_DIM), 0.02),
        'conf_ln_w': 1.0 + nrm((N_CONF, CONF_DIM), 0.02),
        'conf_ln_b': nrm((N_CONF, CONF_DIM), 0.02),
        'conf_w_pw2': nrm((N_CONF, CONF_DIM, D), CONF_DIM ** -0.5),
        'conf_b_pw2': nrm((N_CONF, D), 0.02),
        'final_norm': 1.0 + nrm((D,), 0.02),
    }


def reference(x_prompt, x_sample, state_gdn_recurrent, state_gdn_conv, state_conformer_conv,
              ffn_norm_pre, ffn_pre_w_gate_up, ffn_pre_w_down, mixer_norm, ffn_norm_post,
              ffn_post_w_gate_up, ffn_post_w_down, gdn_w_in, gdn_conv_w, gdn_a_log, gdn_dt_bias,
              gdn_norm_w, gdn_w_out, conf_w_pw1, conf_b_pw1, conf_w_dw, conf_b_dw, conf_ln_w,
              conf_ln_b, conf_w_pw2, conf_b_pw2, final_norm):
    p = dict(ffn_norm_pre=ffn_norm_pre, ffn_pre_w_gate_up=ffn_pre_w_gate_up, ffn_pre_w_down=ffn_pre_w_down,
             mixer_norm=mixer_norm, ffn_norm_post=ffn_norm_post, ffn_post_w_gate_up=ffn_post_w_gate_up,
             ffn_post_w_down=ffn_post_w_down, gdn_w_in=gdn_w_in, gdn_conv_w=gdn_conv_w, gdn_a_log=gdn_a_log,
             gdn_dt_bias=gdn_dt_bias, gdn_norm_w=gdn_norm_w, gdn_w_out=gdn_w_out, conf_w_pw1=conf_w_pw1,
             conf_b_pw1=conf_b_pw1, conf_w_dw=conf_w_dw, conf_b_dw=conf_b_dw, conf_ln_w=conf_ln_w,
             conf_ln_b=conf_ln_b, conf_w_pw2=conf_w_pw2, conf_b_pw2=conf_b_pw2, final_norm=final_norm)
    B = x_prompt.shape[0]
    zS = jnp.zeros((N_DELTA, B, GDN_V_HEADS, GDN_HEAD_K, GDN_HEAD_V), jnp.float32)
    zg = jnp.zeros((N_DELTA, B, GDN_CONV_W - 1, GDN_CONV_DIM), x_prompt.dtype)
    zc = jnp.zeros((N_CONF, B, CONF_KERNEL - 1, CONF_DIM), x_prompt.dtype)
    y_prompt, S_p, gbuf_p, cbuf_p = trunk(x_prompt, zS, zg, zc, p)
    y_sample, S_s, gbuf_s, cbuf_s = trunk(x_sample, state_gdn_recurrent, state_gdn_conv, state_conformer_conv, p)
    return (y_prompt, y_sample, S_p, S_s, gbuf_p, gbuf_s, cbuf_p, cbuf_s)
```

```python
import functools

import jax
import jax.numpy as jnp
from jax import lax
from jax.experimental import pallas as pl
from jax.experimental.pallas import tpu as pltpu

F32 = jnp.float32
BF16 = jnp.bfloat16
EPS = 1e-6
HEAD = 128
CHUNK = 64
LANE = 128
MIB = 1024 * 1024
V7X_VMEM_CAP = 56 * MIB
HIGHEST = lax.Precision.HIGHEST


def _cp(semantics, vmem_mib):
    return pltpu.CompilerParams(dimension_semantics=semantics,
                                vmem_limit_bytes=min(vmem_mib * MIB, V7X_VMEM_CAP))


def _dot(a, b):
    return jnp.dot(a, b, preferred_element_type=F32)


def _dot_nt(a, b):
    return lax.dot_general(a, b, (((1,), (1,)), ((), ())), preferred_element_type=F32)


def _dot_tn(a, b):
    return lax.dot_general(a, b, (((0,), (0,)), ((), ())), preferred_element_type=F32)


def _dot_hi(a, b):
    return jnp.dot(a, b, precision=HIGHEST, preferred_element_type=F32)


def _silu(x):
    return x * jax.nn.sigmoid(x)


def _rms_kernel(x_ref, w_ref, o_ref):
    x = x_ref[...]
    inv = lax.rsqrt(jnp.mean(x * x, axis=-1, keepdims=True) + EPS)
    o_ref[...] = (x * inv * w_ref[...]).astype(o_ref.dtype)


def _rms_norm(x, w_stack, layer, out_dtype, tm, row0=0, rows=None):
    m, d = x.shape
    rows = m if rows is None else rows
    w3 = w_stack.reshape(-1, 1, d)
    blk0 = row0 // tm
    return pl.pallas_call(
        _rms_kernel,
        grid=(rows // tm,),
        in_specs=[pl.BlockSpec((tm, d), lambda i: (i + blk0, 0)),
                  pl.BlockSpec((None, 1, d), lambda i: (layer, 0, 0))],
        out_specs=pl.BlockSpec((tm, d), lambda i: (i, 0)),
        out_shape=jax.ShapeDtypeStruct((rows, d), out_dtype),
        compiler_params=_cp(("parallel",), 40),
    )(x, w3)


def _ws_kernel(*refs, n_w, n_e, epilogue):
    a_ref = refs[0]
    w_refs = refs[1:1 + n_w]
    e_refs = refs[1 + n_w:1 + n_w + n_e]
    o_ref = refs[1 + n_w + n_e]
    wb_refs = refs[2 + n_w + n_e:]

    @pl.when(pl.program_id(1) == 0)
    def _():
        for w_ref, wb_ref in zip(w_refs, wb_refs):
            wb_ref[...] = w_ref[...].astype(BF16)

    a = a_ref[...]
    accs = [_dot(a, wb_ref[...]) for wb_ref in wb_refs]
    o_ref[...] = epilogue(accs, [e_ref[...] for e_ref in e_refs]).astype(o_ref.dtype)


def _ws_matmul(a, w_stack, layer, col_blocks, extras, epilogue, n_out, out_dtype, tm, tn,
               vmem_mib=48):
    m, k = a.shape
    n_w = len(col_blocks)
    in_specs = [pl.BlockSpec((tm, k), lambda j, i: (i, 0))]
    for cb in col_blocks:
        in_specs.append(pl.BlockSpec((None, k, tn), lambda j, i, cb=cb: (layer, 0, j + cb)))
    in_specs += [spec for _, spec in extras]
    kern = functools.partial(_ws_kernel, n_w=n_w, n_e=len(extras), epilogue=epilogue)
    return pl.pallas_call(
        kern,
        grid=(n_out // tn, m // tm),
        in_specs=in_specs,
        out_specs=pl.BlockSpec((tm, tn), lambda j, i: (i, j)),
        out_shape=jax.ShapeDtypeStruct((m, n_out), out_dtype),
        scratch_shapes=[pltpu.VMEM((k, tn), BF16) for _ in range(n_w)],
        compiler_params=_cp(("parallel", "arbitrary"), vmem_mib),
    )(a, *([w_stack] * n_w), *[arr for arr, _ in extras])


def _row_spec(tn, layer, col0_blocks=0):
    return pl.BlockSpec((None, 1, tn), lambda j, i: (layer, 0, j + col0_blocks))


def _tile_spec(tm, tn):
    return pl.BlockSpec((tm, tn), lambda j, i: (i, j))


def _swiglu_epilogue(accs, _):
    g, u = accs
    return _silu(g) * u


def _down_kernel(a_ref, at_ref, w_ref, wt_ref, x_ref, o_ref, *, nk, scale):
    k = pl.program_id(2)

    @pl.when(k == 0)
    def _():
        o_ref[...] = _dot(at_ref[...], wt_ref[...].astype(BF16))

    o_ref[...] += _dot(a_ref[...], w_ref[...].astype(BF16))

    @pl.when(k == nk - 1)
    def _():
        o_ref[...] = x_ref[...] + scale * o_ref[...]


def _down_matmul(h, w_stack, layer, x, scale, tm, tn, tk, tail):
    m, kdim = h.shape
    n = x.shape[1]
    nk = (kdim - tail) // tk
    tail_blk = (nk * tk) // tail
    kern = functools.partial(_down_kernel, nk=nk, scale=scale)
    return pl.pallas_call(
        kern,
        grid=(n // tn, m // tm, nk),
        in_specs=[pl.BlockSpec((tm, tk), lambda j, i, k: (i, k)),
                  pl.BlockSpec((tm, tail), lambda j, i, k: (i, tail_blk)),
                  pl.BlockSpec((None, tk, tn), lambda j, i, k: (layer, k, j)),
                  pl.BlockSpec((None, tail, tn), lambda j, i, k: (layer, tail_blk, j)),
                  pl.BlockSpec((tm, tn), lambda j, i, k: (i, j))],
        out_specs=pl.BlockSpec((tm, tn), lambda j, i, k: (i, j)),
        out_shape=jax.ShapeDtypeStruct((m, n), F32),
        compiler_params=_cp(("parallel", "parallel", "arbitrary"), 48),
    )(h, h, w_stack, w_stack, x)


def _ffn(x, norm_w, w_gu, w_down, layer, t):
    d = x.shape[1]
    f = w_down.shape[1]
    h = _rms_norm(x, norm_w, layer, BF16, t["tm_norm"])
    tn = t["tn_up"]
    hid = _ws_matmul(h, w_gu, layer, (0, f // tn), [], _swiglu_epilogue, f, BF16, t["tm"], tn)
    return _down_matmul(hid, w_down, layer, x, 0.5, t["tm_down"], t["tn_down"], t["tk_down"],
                        t["tail_down"])


def _gates_kernel(a_ref, wb_ref, wa_ref, alog_ref, dtb_ref, beta_ref, g_ref):
    a = a_ref[...]
    b = _dot(a, wb_ref[...].astype(BF16))
    t = _dot(a, wa_ref[...].astype(BF16)) + dtb_ref[...]
    beta_ref[...] = jax.nn.sigmoid(b)
    softplus = jnp.maximum(t, 0.0) + jnp.log1p(jnp.exp(-jnp.abs(t)))
    g_ref[...] = -jnp.exp(alog_ref[...]) * softplus


def _gdn_gates(h, w_b, w_a, a_log, dt_bias, layer, tm):
    m, k = h.shape
    hv = w_b.shape[1]
    vec = pl.BlockSpec((None, 1, hv), lambda i: (layer, 0, 0))
    return pl.pallas_call(
        _gates_kernel,
        grid=(m // tm,),
        in_specs=[pl.BlockSpec((tm, k), lambda i: (i, 0)),
                  pl.BlockSpec((k, hv), lambda i: (0, 0)),
                  pl.BlockSpec((k, hv), lambda i: (0, 0)),
                  vec, vec],
        out_specs=[pl.BlockSpec((tm, hv), lambda i: (i, 0))] * 2,
        out_shape=[jax.ShapeDtypeStruct((m, hv), F32)] * 2,
        compiler_params=_cp(("parallel",), 40),
    )(h, w_b, w_a, a_log.reshape(-1, 1, hv), dt_bias.reshape(-1, 1, hv))


def _gconv_prompt_kernel(x_ref, w_ref, act_ref, buf_ref, pad_ref, *, t, kw):
    tc = x_ref.shape[1]
    pad_ref[0:8, :] = jnp.zeros((8, tc), F32)
    pad_ref[8:8 + t, :] = x_ref[...]
    acc = x_ref[...] * w_ref[kw - 1:kw, :]
    for i in range(kw - 1):
        acc = acc + pad_ref[pl.ds(8 - (kw - 1) + i, t), :] * w_ref[i:i + 1, :]
    act_ref[...] = _silu(acc)
    buf_ref[...] = x_ref[t - (kw - 1):t, :]


def _gconv_prompt(proj, conv_w, layer, b, t, conv_dim, tc):
    kw = conv_w.shape[1]
    kern = functools.partial(_gconv_prompt_kernel, t=t, kw=kw)
    return pl.pallas_call(
        kern,
        grid=(b, conv_dim // tc),
        in_specs=[pl.BlockSpec((t, tc), lambda bi, c: (bi, c)),
                  pl.BlockSpec((None, kw, tc), lambda bi, c: (layer, 0, c))],
        out_specs=[pl.BlockSpec((t, tc), lambda bi, c: (bi, c)),
                   pl.BlockSpec((None, kw - 1, tc), lambda bi, c: (bi, 0, c))],
        out_shape=[jax.ShapeDtypeStruct((b * t, conv_dim), F32),
                   jax.ShapeDtypeStruct((b, kw - 1, conv_dim), F32)],
        scratch_shapes=[pltpu.VMEM((t + 8, tc), F32)],
        compiler_params=_cp(("parallel", "parallel"), 40),
    )(proj, conv_w)


def _gconv_step_kernel(x_ref, buf_ref, w_ref, act_ref, nbuf_ref, *, kw):
    x = x_ref[...]
    acc = x * w_ref[kw - 1:kw, :]
    for i in range(kw - 1):
        row = buf_ref[:, i, :]
        acc = acc + row * w_ref[i:i + 1, :]
        if i >= 1:
            nbuf_ref[:, i - 1, :] = row
    nbuf_ref[:, kw - 2, :] = x
    act_ref[...] = _silu(acc)


def _gconv_step(proj, conv_buf, conv_w, layer, row0, bs, conv_dim, tc):
    kw = conv_w.shape[1]
    kern = functools.partial(_gconv_step_kernel, kw=kw)
    rb = row0 // bs
    return pl.pallas_call(
        kern,
        grid=(conv_dim // tc,),
        in_specs=[pl.BlockSpec((bs, tc), lambda c: (rb, c)),
                  pl.BlockSpec((None, bs, kw - 1, tc), lambda c: (layer, 0, 0, c)),
                  pl.BlockSpec((None, kw, tc), lambda c: (layer, 0, c))],
        out_specs=[pl.BlockSpec((bs, tc), lambda c: (0, c)),
                   pl.BlockSpec((bs, kw - 1, tc), lambda c: (0, 0, c))],
        out_shape=[jax.ShapeDtypeStruct((bs, conv_dim), F32),
                   jax.ShapeDtypeStruct((bs, kw - 1, conv_dim), F32)],
        compiler_params=_cp(("parallel",), 40),
    )(proj, conv_buf, conv_w)


def _l2norm(x):
    return x * lax.rsqrt(jnp.sum(x * x, axis=-1, keepdims=True) + EPS)


def _unit_lower_inverse(low, eye):
    c = low.shape[0]
    x = eye - low
    p = _dot_hi(low, low)
    power = 2
    while True:
        x = x + _dot_hi(x, p)
        power *= 2
        if power >= c:
            return x
        p = _dot_hi(p, p)


def _gated_out(o, z, nw):
    o = o * lax.rsqrt(jnp.mean(o * o, axis=-1, keepdims=True) + EPS) * nw
    return o * _silu(z)


def _gdn_chunk_kernel(q_ref, k_ref, v_ref, z_ref, bcol_ref, gcol_ref, grow_ref, nw_ref,
                      o_ref, s_ref, *, hb):
    c = CHUNK

    @pl.when(pl.program_id(2) == 0)
    def _():
        s_ref[...] = jnp.zeros_like(s_ref)

    row = lax.broadcasted_iota(jnp.int32, (c, c), 0)
    col = lax.broadcasted_iota(jnp.int32, (c, c), 1)
    tril = row >= col
    strict = row > col
    eye = (row == col).astype(F32)
    gcol = _dot_hi(tril.astype(F32), gcol_ref[...])
    grow = _dot_hi(grow_ref[...], (row <= col).astype(F32))
    beta_all = bcol_ref[...]
    nw = nw_ref[...]

    for p in range(hb // 2):
        lanes_qk = slice(p * HEAD, (p + 1) * HEAD)
        qn = _l2norm(q_ref[:, lanes_qk]) * (HEAD ** -0.5)
        kn = _l2norm(k_ref[:, lanes_qk])
        kn16 = kn.astype(BF16)
        qk = _dot_nt(qn.astype(BF16), kn16)
        for r in range(2):
            hh = 2 * p + r
            lanes = slice(hh * HEAD, (hh + 1) * HEAD)
            gc = gcol[:, hh:hh + 1]
            gr = grow[hh:hh + 1, :]
            beta = beta_all[:, hh:hh + 1]
            decay = jnp.where(tril, jnp.exp(jnp.where(tril, gc - gr, 0.0)), 0.0)
            kbeta = kn * beta
            low = jnp.where(strict, _dot_nt(kbeta.astype(BF16), kn16) * decay, 0.0)
            tinv = _unit_lower_inverse(low, eye).astype(BF16)
            egc = jnp.exp(gc)
            u = _dot(tinv, (v_ref[:, lanes] * beta).astype(BF16))
            w = _dot(tinv, (kbeta * egc).astype(BF16))
            a_intra = jnp.where(tril, qk * decay, 0.0)
            s = s_ref[hh]
            s16 = s.astype(BF16)
            v_new = u - _dot(w.astype(BF16), s16)
            v_new16 = v_new.astype(BF16)
            o = _dot((qn * egc).astype(BF16), s16) + _dot(a_intra.astype(BF16), v_new16)
            gl = gc[c - 1:c, :]
            kd = kn * jnp.exp(gl - gc)
            s_ref[hh] = s * jnp.exp(gl) + _dot_tn(kd.astype(BF16), v_new16)
            o_ref[:, lanes] = _gated_out(o, z_ref[:, lanes], nw).astype(o_ref.dtype)


def _gdn_chunk(act, proj, bcol, gcol, grow, norm_w, layer, b, t, hv, key_dim, conv_dim, hb):
    nc = t // CHUNK
    hg = hv // hb
    wqk = (hb // 2) * HEAD
    wv = hb * HEAD
    kern = functools.partial(_gdn_chunk_kernel, hb=hb)
    rowblk = lambda bi, g, n: bi * nc + n
    return pl.pallas_call(
        kern,
        grid=(b, hg, nc),
        in_specs=[pl.BlockSpec((CHUNK, wqk), lambda bi, g, n: (rowblk(bi, g, n), g)),
                  pl.BlockSpec((CHUNK, wqk), lambda bi, g, n: (rowblk(bi, g, n), key_dim // wqk + g)),
                  pl.BlockSpec((CHUNK, wv), lambda bi, g, n: (rowblk(bi, g, n), 2 * key_dim // wv + g)),
                  pl.BlockSpec((CHUNK, wv), lambda bi, g, n: (rowblk(bi, g, n), conv_dim // wv + g)),
                  pl.BlockSpec((None, CHUNK, hb), lambda bi, g, n: (g, rowblk(bi, g, n), 0)),
                  pl.BlockSpec((None, CHUNK, hb), lambda bi, g, n: (g, rowblk(bi, g, n), 0)),
                  pl.BlockSpec((None, None, hb, CHUNK), lambda bi, g, n: (g, rowblk(bi, g, n), 0, 0)),
                  pl.BlockSpec((None, 1, HEAD), lambda bi, g, n: (layer, 0, 0))],
        out_specs=[pl.BlockSpec((CHUNK, wv), lambda bi, g, n: (rowblk(bi, g, n), g)),
                   pl.BlockSpec((None, hb, HEAD, HEAD), lambda bi, g, n: (bi, g, 0, 0))],
        out_shape=[jax.ShapeDtypeStruct((b * t, hv * HEAD), BF16),
                   jax.ShapeDtypeStruct((b, hv, HEAD, HEAD), F32)],
        compiler_params=_cp(("parallel", "parallel", "arbitrary"), 32),
    )(act, act, act, proj, bcol, gcol, grow, norm_w.reshape(-1, 1, HEAD))


def _gdn_step_kernel(q_ref, k_ref, v_ref, z_ref, bcol_ref, gcol_ref, g3_ref, nw_ref, s0_ref,
                     o_ref, s_ref, *, hb, bt):
    rows = lax.broadcasted_iota(jnp.int32, (bt, bt * HEAD), 0)
    cols = lax.broadcasted_iota(jnp.int32, (bt, bt * HEAD), 1)
    bd = (cols // HEAD) == rows
    bd2 = jnp.concatenate([bd, bd], axis=0)
    nw = nw_ref[...]
    beta_all = bcol_ref[...]
    g_all = gcol_ref[...]

    def block_diag(x, mask):
        return jnp.where(mask, jnp.concatenate([x] * bt, axis=1), 0.0).astype(BF16)

    for p in range(hb // 2):
        lanes_qk = slice(p * HEAD, (p + 1) * HEAD)
        qn = _l2norm(q_ref[:, lanes_qk]) * (HEAD ** -0.5)
        kn = _l2norm(k_ref[:, lanes_qk])
        qk = jnp.sum(qn * kn, axis=-1, keepdims=True)
        for r in range(2):
            hh = 2 * p + r
            lanes = slice(hh * HEAD, (hh + 1) * HEAD)
            beta = beta_all[:, hh:hh + 1]
            eg = jnp.exp(g_all[:, hh:hh + 1])
            u = v_ref[:, lanes] * beta
            w = kn * beta * eg
            s3 = s0_ref[:, hh]
            s2 = s3.reshape(bt * HEAD, HEAD).astype(BF16)
            res = _dot(block_diag(jnp.concatenate([w, qn * eg], axis=0), bd2), s2)
            v_new = u - res[:bt]
            o = res[bt:] + qk * v_new
            upd = _dot_tn(block_diag(kn, bd), v_new.astype(BF16))
            s_ref[:, hh] = s3 * jnp.exp(g3_ref[hh]) + upd.reshape(bt, HEAD, HEAD)
            o_ref[:, lanes] = _gated_out(o, z_ref[:, lanes], nw).astype(o_ref.dtype)


def _gdn_step(act, proj, bcol, gcol, g3, norm_w, state, layer, row0, bs, hv, key_dim, conv_dim,
              hb, bt):
    hg = hv // hb
    wqk = (hb // 2) * HEAD
    wv = hb * HEAD
    rb = row0 // bt
    kern = functools.partial(_gdn_step_kernel, hb=hb, bt=bt)
    return pl.pallas_call(
        kern,
        grid=(bs // bt, hg),
        in_specs=[pl.BlockSpec((bt, wqk), lambda i, g: (i, g)),
                  pl.BlockSpec((bt, wqk), lambda i, g: (i, key_dim // wqk + g)),
                  pl.BlockSpec((bt, wv), lambda i, g: (i, 2 * key_dim // wv + g)),
                  pl.BlockSpec((bt, wv), lambda i, g: (rb + i, conv_dim // wv + g)),
                  pl.BlockSpec((None, bt, hb), lambda i, g: (g, i, 0)),
                  pl.BlockSpec((None, bt, hb), lambda i, g: (g, i, 0)),
                  pl.BlockSpec((None, hb, bt, 1, 1), lambda i, g: (g, 0, i, 0, 0)),
                  pl.BlockSpec((None, 1, HEAD), lambda i, g: (layer, 0, 0)),
                  pl.BlockSpec((None, bt, hb, HEAD, HEAD), lambda i, g: (layer, i, g, 0, 0))],
        out_specs=[pl.BlockSpec((bt, wv), lambda i, g: (i, g)),
                   pl.BlockSpec((bt, hb, HEAD, HEAD), lambda i, g: (i, g, 0, 0))],
        out_shape=[jax.ShapeDtypeStruct((bs, hv * HEAD), BF16),
                   jax.ShapeDtypeStruct((bs, hv, HEAD, HEAD), F32)],
        compiler_params=_cp(("parallel", "parallel"), 40),
    )(act, act, act, proj, bcol, gcol, g3, norm_w.reshape(-1, 1, HEAD), state)


def _by_head_group(x, hb):
    r, hv = x.shape
    return jnp.transpose(x.reshape(r, hv // hb, hb), (1, 0, 2))


def _gdn_mixer(x, h, p, j, dims, t):
    b, tlen, bs = dims["b"], dims["t"], dims["bs"]
    mp = b * tlen
    hv = p["gdn_a_log"].shape[1]
    w_in = p["gdn_w_in"]
    conv_dim = p["gdn_conv_w"].shape[2]
    val_dim = hv * HEAD
    key_dim = (conv_dim - val_dim) // 2
    n_main = conv_dim + val_dim
    hb = t["hb"]

    proj = _ws_matmul(h, w_in, j, (0,), [], lambda accs, _: accs[0], n_main, F32, t["tm"],
                      t["tn_in"])
    w_b = w_in[j, :, n_main:n_main + hv]
    w_a = w_in[j, :, n_main + hv:]
    beta, g = _gdn_gates(h, w_b, w_a, p["gdn_a_log"], p["gdn_dt_bias"], j, t["tm"])

    act_p, gbuf_p = _gconv_prompt(proj, p["gdn_conv_w"], j, b, tlen, conv_dim, t["tc_gconv"])
    nc = tlen // CHUNK
    bcol = _by_head_group(beta[:mp], hb)
    gcol = _by_head_group(g[:mp], hb)
    grow = jnp.transpose(gcol.reshape(hv // hb, b * nc, CHUNK, hb), (0, 1, 3, 2))
    o_p, s_p = _gdn_chunk(act_p, proj, bcol, gcol, grow, p["gdn_norm_w"], j, b, tlen, hv, key_dim,
                          conv_dim, hb)

    act_s, gbuf_s = _gconv_step(proj, p["state_gdn_conv"], p["gdn_conv_w"], j, mp, bs, conv_dim,
                                t["tc_gconv"])
    bcol_s = _by_head_group(beta[mp:], hb)
    gcol_s = _by_head_group(g[mp:], hb)
    g3 = jnp.transpose(gcol_s, (0, 2, 1)).reshape(hv // hb, hb, bs, 1, 1)
    o_s, s_s = _gdn_step(act_s, proj, bcol_s, gcol_s, g3, p["gdn_norm_w"],
                         p["state_gdn_recurrent"], j, mp, bs, hv, key_dim, conv_dim, hb, t["bt"])

    o = jnp.concatenate([o_p, o_s], axis=0)
    d = x.shape[1]
    x = _ws_matmul(o, p["gdn_w_out"], j, (0,), [(x, _tile_spec(t["tm"], t["tn_out"]))],
                   lambda accs, e: e[0] + accs[0], d, F32, t["tm"], t["tn_out"])
    return x, s_p, s_s, gbuf_p, gbuf_s


def _layer_norm_silu(hc, lnw, lnb):
    mu = jnp.mean(hc, axis=-1, keepdims=True)
    xc = hc - mu
    y = xc * lax.rsqrt(jnp.mean(xc * xc, axis=-1, keepdims=True) + EPS)
    return _silu(y * lnw + lnb)


def _dwconv_prompt_kernel(cur_ref, prev_ref, w_ref, bdw_ref, lnw_ref, lnb_ref, o_ref, buf_ref,
                          win_ref, conv_ref, *, tt, kw, halo, rc, lc):
    jt = pl.program_id(1)
    d = cur_ref.shape[1]
    hist = kw - 1
    off = halo - hist

    @pl.when(jt == 0)
    def _():
        win_ref[0:halo, :] = jnp.zeros((halo, d), F32)

    @pl.when(jt > 0)
    def _():
        win_ref[0:halo, :] = prev_ref[...]

    win_ref[halo:halo + tt, :] = cur_ref[...]
    for r0 in range(0, tt, rc):
        for c0 in range(0, d, lc):
            lanes = slice(c0, c0 + lc)
            acc = cur_ref[r0:r0 + rc, lanes] * w_ref[hist:kw, lanes]
            for i in range(hist):
                acc = acc + win_ref[r0 + off + i:r0 + off + i + rc, lanes] * w_ref[i:i + 1, lanes]
            conv_ref[r0:r0 + rc, lanes] = acc + bdw_ref[:, lanes]
    o_ref[...] = _layer_norm_silu(conv_ref[...], lnw_ref[...], lnb_ref[...]).astype(o_ref.dtype)

    @pl.when(jt == pl.num_programs(1) - 1)
    def _():
        buf_ref[...] = cur_ref[tt - hist:tt, :]


def _dwconv_prompt(hc, w_dw, b_dw, ln_w, ln_b, layer, b, t, tt, halo=32):
    d = hc.shape[1]
    kw = w_dw.shape[1]
    ntb = t // tt
    hpb = tt // halo
    kern = functools.partial(_dwconv_prompt_kernel, tt=tt, kw=kw, halo=halo, rc=32, lc=512)
    vec = pl.BlockSpec((None, 1, d), lambda bi, jt: (layer, 0, 0))
    return pl.pallas_call(
        kern,
        grid=(b, ntb),
        in_specs=[pl.BlockSpec((tt, d), lambda bi, jt: (bi * ntb + jt, 0)),
                  pl.BlockSpec((halo, d), lambda bi, jt: (jnp.maximum((bi * ntb + jt) * hpb - 1, 0), 0)),
                  pl.BlockSpec((None, kw, d), lambda bi, jt: (layer, 0, 0)),
                  vec, vec, vec],
        out_specs=[pl.BlockSpec((tt, d), lambda bi, jt: (bi * ntb + jt, 0)),
                   pl.BlockSpec((None, kw - 1, d), lambda bi, jt: (bi, 0, 0))],
        out_shape=[jax.ShapeDtypeStruct((b * t, d), BF16),
                   jax.ShapeDtypeStruct((b, kw - 1, d), F32)],
        scratch_shapes=[pltpu.VMEM((tt + halo, d), F32), pltpu.VMEM((tt, d), F32)],
        compiler_params=_cp(("parallel", "arbitrary"), 40),
    )(hc, hc, w_dw, b_dw.reshape(-1, 1, d), ln_w.reshape(-1, 1, d), ln_b.reshape(-1, 1, d))


def _dwconv_step_kernel(x_ref, buf_ref, w_ref, bdw_ref, lnw_ref, lnb_ref, o_ref, nbuf_ref, *, kw):
    x = x_ref[...]
    acc = x * w_ref[kw - 1:kw, :]
    for i in range(kw - 1):
        row = buf_ref[:, i, :]
        acc = acc + row * w_ref[i:i + 1, :]
        if i >= 1:
            nbuf_ref[:, i - 1, :] = row
    nbuf_ref[:, kw - 2, :] = x
    o_ref[...] = _layer_norm_silu(acc + bdw_ref[...], lnw_ref[...], lnb_ref[...]).astype(o_ref.dtype)


def _dwconv_step(hc, conv_buf, w_dw, b_dw, ln_w, ln_b, layer, row0, bs, bt):
    d = hc.shape[1]
    kw = w_dw.shape[1]
    rb = row0 // bt
    kern = functools.partial(_dwconv_step_kernel, kw=kw)
    vec = pl.BlockSpec((None, 1, d), lambda i: (layer, 0, 0))
    return pl.pallas_call(
        kern,
        grid=(bs // bt,),
        in_specs=[pl.BlockSpec((bt, d), lambda i: (rb + i, 0)),
                  pl.BlockSpec((None, bt, kw - 1, d), lambda i: (layer, i, 0, 0)),
                  pl.BlockSpec((None, kw, d), lambda i: (layer, 0, 0)),
                  vec, vec, vec],
        out_specs=[pl.BlockSpec((bt, d), lambda i: (i, 0)),
                   pl.BlockSpec((bt, kw - 1, d), lambda i: (i, 0, 0))],
        out_shape=[jax.ShapeDtypeStruct((bs, d), BF16),
                   jax.ShapeDtypeStruct((bs, kw - 1, d), F32)],
        compiler_params=_cp(("parallel",), 40),
    )(hc, conv_buf, w_dw, b_dw.reshape(-1, 1, d), ln_w.reshape(-1, 1, d), ln_b.reshape(-1, 1, d))


def _glu_epilogue(accs, e):
    return (accs[0] + e[0]) * jax.nn.sigmoid(accs[1] + e[1])


def _conf_mixer(x, h, p, j, dims, t):
    b, tlen, bs = dims["b"], dims["t"], dims["bs"]
    mp = b * tlen
    d = x.shape[1]
    cd = p["conf_w_dw"].shape[2]
    tn = t["tn_pw1"]
    b1 = p["conf_b_pw1"].reshape(-1, 1, 2 * cd)
    hc = _ws_matmul(h, p["conf_w_pw1"], j, (0, cd // tn),
                    [(b1, _row_spec(tn, j)), (b1, _row_spec(tn, j, cd // tn))],
                    _glu_epilogue, cd, F32, t["tm"], tn)
    y_p, cbuf_p = _dwconv_prompt(hc, p["conf_w_dw"], p["conf_b_dw"], p["conf_ln_w"], p["conf_ln_b"],
                                 j, b, tlen, t["tt_dw"])
    y_s, cbuf_s = _dwconv_step(hc, p["state_conformer_conv"], p["conf_w_dw"], p["conf_b_dw"],
                               p["conf_ln_w"], p["conf_ln_b"], j, mp, bs, t["bt"])
    y = jnp.concatenate([y_p, y_s], axis=0)
    b2 = p["conf_b_pw2"].reshape(-1, 1, d)
    tn2 = t["tn_out"]
    x = _ws_matmul(y, p["conf_w_pw2"], j, (0,),
                   [(b2, _row_spec(tn2, j)), (x, _tile_spec(t["tm"], tn2))],
                   lambda accs, e: e[1] + (accs[0] + e[0]), d, F32, t["tm"], tn2)
    return x, cbuf_p, cbuf_s


def _tiles(m, d, f):
    if m == 8320 and d == 4096 and f == 11008:
        return dict(tm=1040, tm_norm=520, tn_up=256, tm_down=1040, tn_down=1024, tk_down=896,
                    tail_down=256, tn_in=512, tn_out=512, tn_pw1=256, tc_gconv=256, tt_dw=128,
                    hb=4, bt=8)
    return dict(tm=m, tm_norm=m, tn_up=256, tm_down=m, tn_down=256, tk_down=256,
                tail_down=256, tn_in=256, tn_out=256, tn_pw1=256, tc_gconv=256, tt_dw=64,
                hb=4, bt=8)


def _trunk(p, depth):
    xp, xs = p["x_prompt"], p["x_sample"]
    b, tlen, d = xp.shape
    bs = xs.shape[0]
    mp = b * tlen
    dims = dict(b=b, t=tlen, bs=bs)
    x = jnp.concatenate([xp.reshape(mp, d), xs.reshape(bs, d)], axis=0)
    f = p["ffn_pre_w_down"].shape[1]
    t = _tiles(x.shape[0], d, f)

    s_p, s_s, gb_p, gb_s, cb_p, cb_s = [], [], [], [], [], []
    for i in range(depth):
        x = _ffn(x, p["ffn_norm_pre"], p["ffn_pre_w_gate_up"], p["ffn_pre_w_down"], i, t)
        h = _rms_norm(x, p["mixer_norm"], i, BF16, t["tm_norm"])
        j = i // 2
        if i % 2 == 0:
            x, sp, ss, gp, gs = _gdn_mixer(x, h, p, j, dims, t)
            s_p.append(sp); s_s.append(ss); gb_p.append(gp); gb_s.append(gs)
        else:
            x, cp, cs = _conf_mixer(x, h, p, j, dims, t)
            cb_p.append(cp); cb_s.append(cs)
        x = _ffn(x, p["ffn_norm_post"], p["ffn_post_w_gate_up"], p["ffn_post_w_down"], i, t)

    fn = p["final_norm"].reshape(1, d)
    tmf = 512 if mp % 512 == 0 else mp
    y_p = _rms_norm(x, fn, 0, F32, tmf, 0, mp)
    y_s = _rms_norm(x, fn, 0, F32, bs, mp, bs)
    return (y_p.reshape(b, tlen, d), y_s.reshape(bs, 1, d), jnp.stack(s_p), jnp.stack(s_s),
            jnp.stack(gb_p), jnp.stack(gb_s), jnp.stack(cb_p), jnp.stack(cb_s))


def kernel(x_prompt, x_sample, state_gdn_recurrent, state_gdn_conv, state_conformer_conv,
           ffn_norm_pre, ffn_pre_w_gate_up, ffn_pre_w_down, mixer_norm, ffn_norm_post,
           ffn_post_w_gate_up, ffn_post_w_down, gdn_w_in, gdn_conv_w, gdn_a_log, gdn_dt_bias,
           gdn_norm_w, gdn_w_out, conf_w_pw1, conf_b_pw1, conf_w_dw, conf_b_dw, conf_ln_w,
           conf_ln_b, conf_w_pw2, conf_b_pw2, final_norm):
    p = dict(x_prompt=x_prompt, x_sample=x_sample, state_gdn_recurrent=state_gdn_recurrent,
             state_gdn_conv=state_gdn_conv, state_conformer_conv=state_conformer_conv,
             ffn_norm_pre=ffn_norm_pre, ffn_pre_w_gate_up=ffn_pre_w_gate_up,
             ffn_pre_w_down=ffn_pre_w_down, mixer_norm=mixer_norm, ffn_norm_post=ffn_norm_post,
             ffn_post_w_gate_up=ffn_post_w_gate_up, ffn_post_w_down=ffn_post_w_down,
             gdn_w_in=gdn_w_in, gdn_conv_w=gdn_conv_w, gdn_a_log=gdn_a_log,
             gdn_dt_bias=gdn_dt_bias, gdn_norm_w=gdn_norm_w, gdn_w_out=gdn_w_out,
             conf_w_pw1=conf_w_pw1, conf_b_pw1=conf_b_pw1, conf_w_dw=conf_w_dw,
             conf_b_dw=conf_b_dw, conf_ln_w=conf_ln_w, conf_ln_b=conf_ln_b,
             conf_w_pw2=conf_w_pw2, conf_b_pw2=conf_b_pw2, final_norm=final_norm)
    return _trunk(p, ffn_norm_pre.shape[0])
```

```python
import functools

import jax
import jax.numpy as jnp
from jax import lax
from jax.experimental import pallas as pl
from jax.experimental.pallas import tpu as pltpu

F32 = jnp.float32
BF16 = jnp.bfloat16
EPS = 1e-6
HEAD = 128
CHUNK = 64
SUBLANES = 8
MIB = 1024 * 1024
V7X_VMEM_CAP = 56 * MIB
HIGHEST = lax.Precision.HIGHEST


def _cp(semantics, vmem_mib):
    return pltpu.CompilerParams(dimension_semantics=semantics,
                                vmem_limit_bytes=min(vmem_mib * MIB, V7X_VMEM_CAP))


def _dot(a, b):
    return jnp.dot(a, b, preferred_element_type=F32)


def _dot_nt(a, b):
    return lax.dot_general(a, b, (((1,), (1,)), ((), ())), preferred_element_type=F32)


def _dot_tn(a, b):
    return lax.dot_general(a, b, (((0,), (0,)), ((), ())), preferred_element_type=F32)


def _dot_hi(a, b):
    return jnp.dot(a, b, precision=HIGHEST, preferred_element_type=F32)


def _silu(x):
    return x * jax.nn.sigmoid(x)


def _rms_kernel(x_ref, w_ref, o_ref):
    x = x_ref[...]
    inv = lax.rsqrt(jnp.mean(x * x, axis=-1, keepdims=True) + EPS)
    o_ref[...] = (x * inv * w_ref[...]).astype(o_ref.dtype)


def _rms_norm(x, w_stack, layer, out_dtype, tm, row0=0, rows=None):
    m, d = x.shape
    rows = m if rows is None else rows
    w3 = w_stack.reshape(-1, 1, d)
    blk0 = row0 // tm
    return pl.pallas_call(
        _rms_kernel,
        grid=(rows // tm,),
        in_specs=[pl.BlockSpec((tm, d), lambda i: (i + blk0, 0)),
                  pl.BlockSpec((None, 1, d), lambda i: (layer, 0, 0))],
        out_specs=pl.BlockSpec((tm, d), lambda i: (i, 0)),
        out_shape=jax.ShapeDtypeStruct((rows, d), out_dtype),
        compiler_params=_cp(("parallel",), 40),
        name="rms_norm",
    )(x, w3)


def _ws_kernel(*refs, n_w, n_e, epilogue, transposed):
    a_ref = refs[0]
    w_refs = refs[1:1 + n_w]
    e_refs = refs[1 + n_w:1 + n_w + n_e]
    o_ref = refs[1 + n_w + n_e]
    wb_refs = refs[2 + n_w + n_e:]

    @pl.when(pl.program_id(1) == 0)
    def _():
        for w_ref, wb_ref in zip(w_refs, wb_refs):
            wb_ref[...] = w_ref[...].astype(BF16)

    a = a_ref[...]
    mm = _dot_nt if transposed else _dot
    accs = [mm(a, wb_ref[...]) for wb_ref in wb_refs]
    o_ref[...] = epilogue(accs, [e_ref[...] for e_ref in e_refs]).astype(o_ref.dtype)


def _ws_matmul(a, w_stack, layer, col_blocks, extras, epilogue, n_out, out_dtype, tm, tn,
               name, transposed=False, vmem_mib=48):
    m, k = a.shape
    n_w = len(col_blocks)
    in_specs = [pl.BlockSpec((tm, k), lambda j, i: (i, 0))]
    for cb in col_blocks:
        if transposed:
            in_specs.append(pl.BlockSpec((None, tn, k), lambda j, i, cb=cb: (layer, j + cb, 0)))
        else:
            in_specs.append(pl.BlockSpec((None, k, tn), lambda j, i, cb=cb: (layer, 0, j + cb)))
    in_specs += [spec for _, spec in extras]
    kern = functools.partial(_ws_kernel, n_w=n_w, n_e=len(extras), epilogue=epilogue,
                             transposed=transposed)
    wb_shape = (tn, k) if transposed else (k, tn)
    return pl.pallas_call(
        kern,
        grid=(n_out // tn, m // tm),
        in_specs=in_specs,
        out_specs=pl.BlockSpec((tm, tn), lambda j, i: (i, j)),
        out_shape=jax.ShapeDtypeStruct((m, n_out), out_dtype),
        scratch_shapes=[pltpu.VMEM(wb_shape, BF16) for _ in range(n_w)],
        compiler_params=_cp(("parallel", "arbitrary"), vmem_mib),
        name=name,
    )(a, *([w_stack] * n_w), *[arr for arr, _ in extras])


def _row_spec(tn, layer, col0_blocks=0):
    return pl.BlockSpec((None, 1, tn), lambda j, i: (layer, 0, j + col0_blocks))


def _tile_spec(tm, tn):
    return pl.BlockSpec((tm, tn), lambda j, i: (i, j))


def _swiglu_epilogue(accs, _):
    g, u = accs
    return _silu(g) * u


def _down_kernel(a_ref, w_ref, x_ref, o_ref, wb_ref, *, nkw, tkw, scale):
    s = pl.program_id(1)

    @pl.when(s < nkw)
    def _():
        off = pl.multiple_of(s * tkw, 16)
        wb_ref[pl.ds(off, tkw), :] = w_ref[...].astype(BF16)

    @pl.when(s >= nkw)
    def _():
        o_ref[...] = x_ref[...] + scale * _dot(a_ref[...], wb_ref[...])


def _down_matmul(h, w_stack, layer, x, scale, tm, tn, tkw):
    m, kdim = h.shape
    n = x.shape[1]
    nkw = kdim // tkw
    row = lambda s: jnp.maximum(s - nkw, 0)
    kern = functools.partial(_down_kernel, nkw=nkw, tkw=tkw, scale=scale)
    return pl.pallas_call(
        kern,
        grid=(n // tn, nkw + m // tm),
        in_specs=[pl.BlockSpec((tm, kdim), lambda j, s: (row(s), 0)),
                  pl.BlockSpec((None, tkw, tn), lambda j, s: (layer, jnp.minimum(s, nkw - 1), j)),
                  pl.BlockSpec((tm, tn), lambda j, s: (row(s), j))],
        out_specs=pl.BlockSpec((tm, tn), lambda j, s: (row(s), j)),
        out_shape=jax.ShapeDtypeStruct((m, n), F32),
        scratch_shapes=[pltpu.VMEM((kdim, tn), BF16)],
        compiler_params=_cp(("parallel", "arbitrary"), 52),
        name="ffn_down",
    )(h, w_stack, x)


def _ffn(x, norm_w, w_gu, w_down, layer, t):
    f = w_down.shape[1]
    h = _rms_norm(x, norm_w, layer, BF16, t["tm_norm"])
    tn = t["tn_up"]
    hid = _ws_matmul(h, w_gu, layer, (0, f // tn), [], _swiglu_epilogue, f, BF16, t["tm"], tn,
                     "ffn_up")
    return _down_matmul(hid, w_down, layer, x, 0.5, t["tm_down"], t["tn_down"], t["tkw_down"])


def _gates_kernel(a_ref, wb_ref, wa_ref, alog_ref, dtb_ref, beta_ref, gc_ref, *, mp, tm):
    a = a_ref[...]
    b = _dot_nt(a, wb_ref[...].astype(BF16))
    t = _dot_nt(a, wa_ref[...].astype(BF16)) + dtb_ref[...]
    beta_ref[...] = jax.nn.sigmoid(b)
    softplus = jnp.maximum(t, 0.0) + jnp.log1p(jnp.exp(-jnp.abs(t)))
    g = -jnp.exp(alog_ref[...]) * softplus
    r0 = pl.program_id(0) * tm
    row = lax.broadcasted_iota(jnp.int32, (tm, tm), 0) + r0
    col = lax.broadcasted_iota(jnp.int32, (tm, tm), 1) + r0
    chunk_of = lambda i: jnp.where(i < mp, i // CHUNK, i + mp)
    mask = jnp.where(chunk_of(row) == chunk_of(col), jnp.where(col <= row, 1.0, 0.0), 0.0)
    gc_ref[...] = _dot_hi(mask, g)


def _gdn_gates(h, wt_b, wt_a, a_log, dt_bias, layer, mp, tm):
    m, k = h.shape
    hv = wt_b.shape[0]
    vec = pl.BlockSpec((None, 1, hv), lambda i: (layer, 0, 0))
    kern = functools.partial(_gates_kernel, mp=mp, tm=tm)
    return pl.pallas_call(
        kern,
        grid=(m // tm,),
        in_specs=[pl.BlockSpec((tm, k), lambda i: (i, 0)),
                  pl.BlockSpec((hv, k), lambda i: (0, 0)),
                  pl.BlockSpec((hv, k), lambda i: (0, 0)),
                  vec, vec],
        out_specs=[pl.BlockSpec((tm, hv), lambda i: (i, 0))] * 2,
        out_shape=[jax.ShapeDtypeStruct((m, hv), F32)] * 2,
        compiler_params=_cp(("parallel",), 40),
        name="gdn_gates",
    )(h, wt_b, wt_a, a_log.reshape(-1, 1, hv), dt_bias.reshape(-1, 1, hv))


def _gconv_prompt_kernel(x_ref, w_ref, act_ref, buf_ref, pad_ref, *, t, kw):
    tc = x_ref.shape[1]
    pad_ref[0:SUBLANES, :] = jnp.zeros((SUBLANES, tc), F32)
    pad_ref[SUBLANES:SUBLANES + t, :] = x_ref[...]
    acc = x_ref[...] * w_ref[kw - 1:kw, :]
    for i in range(kw - 1):
        acc = acc + pad_ref[pl.ds(SUBLANES - (kw - 1) + i, t), :] * w_ref[i:i + 1, :]
    act_ref[...] = _silu(acc)
    buf_ref[...] = x_ref[t - (kw - 1):t, :]


def _gconv_prompt(proj, conv_w, layer, b, t, conv_dim, tc):
    kw = conv_w.shape[1]
    kern = functools.partial(_gconv_prompt_kernel, t=t, kw=kw)
    return pl.pallas_call(
        kern,
        grid=(b, conv_dim // tc),
        in_specs=[pl.BlockSpec((t, tc), lambda bi, c: (bi, c)),
                  pl.BlockSpec((None, kw, tc), lambda bi, c: (layer, 0, c))],
        out_specs=[pl.BlockSpec((t, tc), lambda bi, c: (bi, c)),
                   pl.BlockSpec((None, kw - 1, tc), lambda bi, c: (bi, 0, c))],
        out_shape=[jax.ShapeDtypeStruct((b * t, conv_dim), F32),
                   jax.ShapeDtypeStruct((b, kw - 1, conv_dim), F32)],
        scratch_shapes=[pltpu.VMEM((t + SUBLANES, tc), F32)],
        compiler_params=_cp(("parallel", "parallel"), 40),
        name="gdn_conv_prompt",
    )(proj, conv_w)


def _gconv_step_kernel(x_ref, buf_ref, w_ref, act_ref, nbuf_ref, *, kw):
    x = x_ref[...]
    acc = x * w_ref[kw - 1:kw, :]
    for i in range(kw - 1):
        row = buf_ref[i]
        acc = acc + row * w_ref[i:i + 1, :]
        if i >= 1:
            nbuf_ref[i - 1] = row
    nbuf_ref[kw - 2] = x
    act_ref[...] = _silu(acc)


def _gconv_step(proj, conv_buf_t, conv_w, layer, row0, bs, conv_dim, tc):
    kw = conv_w.shape[1]
    kern = functools.partial(_gconv_step_kernel, kw=kw)
    rb = row0 // bs
    return pl.pallas_call(
        kern,
        grid=(conv_dim // tc,),
        in_specs=[pl.BlockSpec((bs, tc), lambda c: (rb, c)),
                  pl.BlockSpec((None, kw - 1, bs, tc), lambda c: (layer, 0, 0, c)),
                  pl.BlockSpec((None, kw, tc), lambda c: (layer, 0, c))],
        out_specs=[pl.BlockSpec((bs, tc), lambda c: (0, c)),
                   pl.BlockSpec((kw - 1, bs, tc), lambda c: (0, 0, c))],
        out_shape=[jax.ShapeDtypeStruct((bs, conv_dim), F32),
                   jax.ShapeDtypeStruct((kw - 1, bs, conv_dim), F32)],
        compiler_params=_cp(("parallel",), 40),
        name="gdn_conv_step",
    )(proj, conv_buf_t, conv_w)


def _l2norm(x):
    return x * lax.rsqrt(jnp.sum(x * x, axis=-1, keepdims=True) + EPS)


def _gated_out(o, z, nw):
    o = o * lax.rsqrt(jnp.mean(o * o, axis=-1, keepdims=True) + EPS) * nw
    return o * _silu(z)


def _gdn_prep_kernel(q_ref, k_ref, v_ref, bcol_ref, gcol_ref, grow_ref,
                     u_ref, wq_ref, kd_ref, a_ref, egl_ref, *, hb, cb):
    c = CHUNK
    row = lax.broadcasted_iota(jnp.int32, (c, 2 * c), 0)
    col2 = lax.broadcasted_iota(jnp.int32, (c, 2 * c), 1)
    left = col2 < c
    col = jnp.where(left, col2, col2 - c)
    tril2 = row >= col
    strict2 = row > col
    eye_left = jnp.where(row == col2, 1.0, 0.0)
    zeros_top = jnp.zeros((c, 2 * c), BF16)
    tril = (lax.broadcasted_iota(jnp.int32, (c, c), 0) >= lax.broadcasted_iota(jnp.int32, (c, c), 1))

    qk_units = [(ci, p) for ci in range(cb) for p in range(hb // 2)]
    units = [(ci, hh) for ci in range(cb) for hh in range(hb)]
    rows_of = lambda ci: slice(ci * c, (ci + 1) * c)
    lanes_of = lambda i: slice(i * HEAD, (i + 1) * HEAD)

    qn, kn, kq = {}, {}, {}
    for ci, p in qk_units:
        qn[ci, p] = _l2norm(q_ref[rows_of(ci), lanes_of(p)]) * (HEAD ** -0.5)
        kn[ci, p] = _l2norm(k_ref[rows_of(ci), lanes_of(p)])
    for ci, p in qk_units:
        kn16 = kn[ci, p].astype(BF16)
        kq[ci, p] = _dot_nt(jnp.concatenate([kn16, qn[ci, p].astype(BF16)], axis=0),
                            jnp.concatenate([kn16, kn16], axis=0))

    gc, beta, decay2, pair = {}, {}, {}, {}
    for ci, hh in units:
        gc[ci, hh] = gcol_ref[rows_of(ci), hh:hh + 1]
        beta[ci, hh] = bcol_ref[rows_of(ci), hh:hh + 1]
        gr2 = grow_ref[ci, hh:hh + 1, :]
        decay2[ci, hh] = jnp.where(tril2, jnp.exp(jnp.where(tril2, gc[ci, hh] - gr2, 0.0)), 0.0)
        kk2 = kq[ci, hh // 2][:c]
        neg_low2 = jnp.where(strict2, -(kk2 * beta[ci, hh]) * decay2[ci, hh], 0.0)
        pair[ci, hh] = jnp.where(left, eye_left, neg_low2)

    power = 1
    while power < c:
        prod = {}
        for key in units:
            pair16 = pair[key].astype(BF16)
            prod[key] = _dot(pair16, jnp.concatenate([zeros_top, pair16], axis=0))
        for key in units:
            pair[key] = prod[key] + jnp.where(left, pair[key], 0.0)
        power *= 2

    egc, uw = {}, {}
    for ci, hh in units:
        egc[ci, hh] = jnp.exp(gc[ci, hh])
        rhs = jnp.concatenate([v_ref[rows_of(ci), lanes_of(hh)] * beta[ci, hh],
                               kn[ci, hh // 2] * (beta[ci, hh] * egc[ci, hh])], axis=1)
        rhs16 = rhs.astype(BF16)
        tinv16 = pair[ci, hh].astype(BF16)
        uw[ci, hh] = _dot(tinv16, jnp.concatenate([rhs16, jnp.zeros_like(rhs16)], axis=0))

    for ci, hh in units:
        lanes = lanes_of(hh)
        u_ref[ci, :, lanes] = uw[ci, hh][:, :HEAD]
        wq_ref[ci, 0:c, lanes] = uw[ci, hh][:, HEAD:].astype(BF16)
        wq_ref[ci, c:2 * c, lanes] = (qn[ci, hh // 2] * egc[ci, hh]).astype(BF16)
        qk = kq[ci, hh // 2][c:, :c]
        a_ref[ci, hh] = jnp.where(tril, qk * decay2[ci, hh][:, :c], 0.0).astype(BF16)
        gl = gc[ci, hh][c - 1:c, :]
        kd_ref[ci, :, lanes] = (kn[ci, hh // 2] * jnp.exp(gl - gc[ci, hh])).astype(BF16)
        egl_ref[ci, :, lanes] = jnp.broadcast_to(jnp.exp(gl), (SUBLANES, HEAD))


def _gdn_prep(act, bcol, gcol, grow2, b, t, hv, key_dim, hb, cb):
    nch = b * t // CHUNK
    hg = hv // hb
    wqk = (hb // 2) * HEAD
    wv = hb * HEAD
    c = CHUNK
    kern = functools.partial(_gdn_prep_kernel, hb=hb, cb=cb)
    return pl.pallas_call(
        kern,
        grid=(nch // cb, hg),
        in_specs=[pl.BlockSpec((cb * c, wqk), lambda i, g: (i, g)),
                  pl.BlockSpec((cb * c, wqk), lambda i, g: (i, key_dim // wqk + g)),
                  pl.BlockSpec((cb * c, wv), lambda i, g: (i, 2 * key_dim // wv + g)),
                  pl.BlockSpec((None, cb * c, hb), lambda i, g: (g, i, 0)),
                  pl.BlockSpec((None, cb * c, hb), lambda i, g: (g, i, 0)),
                  pl.BlockSpec((None, cb, hb, 2 * c), lambda i, g: (g, i, 0, 0))],
        out_specs=[pl.BlockSpec((cb, c, wv), lambda i, g: (i, 0, g)),
                   pl.BlockSpec((cb, 2 * c, wv), lambda i, g: (i, 0, g)),
                   pl.BlockSpec((cb, c, wv), lambda i, g: (i, 0, g)),
                   pl.BlockSpec((cb, hb, c, c), lambda i, g: (i, g, 0, 0)),
                   pl.BlockSpec((cb, SUBLANES, wv), lambda i, g: (i, 0, g))],
        out_shape=[jax.ShapeDtypeStruct((nch, c, hv * HEAD), F32),
                   jax.ShapeDtypeStruct((nch, 2 * c, hv * HEAD), BF16),
                   jax.ShapeDtypeStruct((nch, c, hv * HEAD), BF16),
                   jax.ShapeDtypeStruct((nch, hv, c, c), BF16),
                   jax.ShapeDtypeStruct((nch, SUBLANES, hv * HEAD), F32)],
        compiler_params=_cp(("parallel", "parallel"), 40),
        name="gdn_prep",
    )(act, act, act, bcol, gcol, grow2)


def _gdn_scan_kernel(u_ref, wq_ref, kd_ref, a_ref, egl_ref, z_ref, nw_ref, o_ref, s_ref, *, hb, nc):
    c = CHUNK

    @pl.when(pl.program_id(2) == 0)
    def _():
        s_ref[...] = jnp.zeros_like(s_ref)

    nw = nw_ref[...]

    def body(n, carry):
        r0 = pl.multiple_of(n * c, c)
        heads = range(hb)
        lanes = [slice(hh * HEAD, (hh + 1) * HEAD) for hh in heads]
        s = [s_ref[hh] for hh in heads]
        res = [_dot(wq_ref[n, :, lanes[hh]], s[hh].astype(BF16)) for hh in heads]
        v16 = [(u_ref[n, :, lanes[hh]] - res[hh][:c]).astype(BF16) for hh in heads]
        upd = [_dot_tn(kd_ref[n, :, lanes[hh]], v16[hh]) for hh in heads]
        o_intra = [_dot(a_ref[n, hh], v16[hh]) for hh in heads]
        for hh in heads:
            s_ref[hh] = s[hh] * egl_ref[n, 0:1, lanes[hh]] + upd[hh]
        for hh in heads:
            o = res[hh][c:] + o_intra[hh]
            o_ref[pl.ds(r0, c), lanes[hh]] = _gated_out(o, z_ref[pl.ds(r0, c), lanes[hh]], nw).astype(o_ref.dtype)
        return carry

    lax.fori_loop(0, nc, body, 0)


def _gdn_scan(u3, wq3, kd3, a4, egl3, proj, norm_w, layer, b, t, hv, conv_dim, hb, nc):
    ntb = (t // CHUNK) // nc
    hg = hv // hb
    wv = hb * HEAD
    c = CHUNK
    kern = functools.partial(_gdn_scan_kernel, hb=hb, nc=nc)
    blk = lambda bi, g, tb: bi * ntb + tb
    return pl.pallas_call(
        kern,
        grid=(b, hg, ntb),
        in_specs=[pl.BlockSpec((nc, c, wv), lambda bi, g, tb: (blk(bi, g, tb), 0, g)),
                  pl.BlockSpec((nc, 2 * c, wv), lambda bi, g, tb: (blk(bi, g, tb), 0, g)),
                  pl.BlockSpec((nc, c, wv), lambda bi, g, tb: (blk(bi, g, tb), 0, g)),
                  pl.BlockSpec((nc, hb, c, c), lambda bi, g, tb: (blk(bi, g, tb), g, 0, 0)),
                  pl.BlockSpec((nc, SUBLANES, wv), lambda bi, g, tb: (blk(bi, g, tb), 0, g)),
                  pl.BlockSpec((nc * c, wv), lambda bi, g, tb: (blk(bi, g, tb), conv_dim // wv + g)),
                  pl.BlockSpec((None, 1, HEAD), lambda bi, g, tb: (layer, 0, 0))],
        out_specs=[pl.BlockSpec((nc * c, wv), lambda bi, g, tb: (blk(bi, g, tb), g)),
                   pl.BlockSpec((None, hb, HEAD, HEAD), lambda bi, g, tb: (bi, g, 0, 0))],
        out_shape=[jax.ShapeDtypeStruct((b * t, hv * HEAD), BF16),
                   jax.ShapeDtypeStruct((b, hv, HEAD, HEAD), F32)],
        compiler_params=_cp(("parallel", "parallel", "arbitrary"), 40),
        name="gdn_scan",
    )(u3, wq3, kd3, a4, egl3, proj, norm_w.reshape(-1, 1, HEAD))


def _gdn_step_kernel(q_ref, k_ref, v_ref, z_ref, bcol_ref, gcol_ref, g3_ref, nw_ref, s0_ref,
                     o_ref, s_ref, *, hb, bt):
    rows = lax.broadcasted_iota(jnp.int32, (bt, bt * HEAD), 0)
    cols = lax.broadcasted_iota(jnp.int32, (bt, bt * HEAD), 1)
    bd = (cols // HEAD) == rows
    bd2 = jnp.concatenate([bd, bd], axis=0)
    nw = nw_ref[...]
    beta_all = bcol_ref[...]
    g_all = gcol_ref[...]

    def block_diag(x, mask):
        return jnp.where(mask, jnp.concatenate([x] * bt, axis=1), 0.0).astype(BF16)

    for p in range(hb // 2):
        lanes_qk = slice(p * HEAD, (p + 1) * HEAD)
        qn = _l2norm(q_ref[:, lanes_qk]) * (HEAD ** -0.5)
        kn = _l2norm(k_ref[:, lanes_qk])
        qk = jnp.sum(qn * kn, axis=-1, keepdims=True)
        for r in range(2):
            hh = 2 * p + r
            lanes = slice(hh * HEAD, (hh + 1) * HEAD)
            beta = beta_all[:, hh:hh + 1]
            eg = jnp.exp(g_all[:, hh:hh + 1])
            u = v_ref[:, lanes] * beta
            w = kn * beta * eg
            s3 = s0_ref[:, hh]
            s2 = s3.reshape(bt * HEAD, HEAD).astype(BF16)
            res = _dot(block_diag(jnp.concatenate([w, qn * eg], axis=0), bd2), s2)
            v_new = u - res[:bt]
            o = res[bt:] + qk * v_new
            upd = _dot_tn(block_diag(kn, bd), v_new.astype(BF16))
            s_ref[:, hh] = s3 * jnp.exp(g3_ref[hh]) + upd.reshape(bt, HEAD, HEAD)
            o_ref[:, lanes] = _gated_out(o, z_ref[:, lanes], nw).astype(o_ref.dtype)


def _gdn_step(act, proj, bcol, gcol, g3, norm_w, state, layer, row0, bs, hv, key_dim, conv_dim,
              hb, bt):
    hg = hv // hb
    wqk = (hb // 2) * HEAD
    wv = hb * HEAD
    rb = row0 // bt
    kern = functools.partial(_gdn_step_kernel, hb=hb, bt=bt)
    return pl.pallas_call(
        kern,
        grid=(bs // bt, hg),
        in_specs=[pl.BlockSpec((bt, wqk), lambda i, g: (i, g)),
                  pl.BlockSpec((bt, wqk), lambda i, g: (i, key_dim // wqk + g)),
                  pl.BlockSpec((bt, wv), lambda i, g: (i, 2 * key_dim // wv + g)),
                  pl.BlockSpec((bt, wv), lambda i, g: (rb + i, conv_dim // wv + g)),
                  pl.BlockSpec((None, bt, hb), lambda i, g: (g, i, 0)),
                  pl.BlockSpec((None, bt, hb), lambda i, g: (g, i, 0)),
                  pl.BlockSpec((None, hb, bt, 1, 1), lambda i, g: (g, 0, i, 0, 0)),
                  pl.BlockSpec((None, 1, HEAD), lambda i, g: (layer, 0, 0)),
                  pl.BlockSpec((None, bt, hb, HEAD, HEAD), lambda i, g: (layer, i, g, 0, 0))],
        out_specs=[pl.BlockSpec((bt, wv), lambda i, g: (i, g)),
                   pl.BlockSpec((bt, hb, HEAD, HEAD), lambda i, g: (i, g, 0, 0))],
        out_shape=[jax.ShapeDtypeStruct((bs, hv * HEAD), BF16),
                   jax.ShapeDtypeStruct((bs, hv, HEAD, HEAD), F32)],
        compiler_params=_cp(("parallel", "parallel"), 40),
        name="gdn_step",
    )(act, act, act, proj, bcol, gcol, g3, norm_w.reshape(-1, 1, HEAD), state)


def _by_head_group(x, hb):
    r, hv = x.shape
    return jnp.transpose(x.reshape(r, hv // hb, hb), (1, 0, 2))


def _gdn_mixer(x, h, p, j, dims, t):
    b, tlen, bs = dims["b"], dims["t"], dims["bs"]
    mp = b * tlen
    hv = p["gdn_a_log"].shape[1]
    conv_dim = p["gdn_conv_w"].shape[2]
    val_dim = hv * HEAD
    key_dim = (conv_dim - val_dim) // 2
    n_main = conv_dim + val_dim
    hb = t["hb"]
    wt_in = jnp.transpose(p["gdn_w_in"], (0, 2, 1))

    proj = _ws_matmul(h, wt_in, j, (0,), [], lambda accs, _: accs[0], n_main, F32, t["tm"],
                      t["tn_in"], "gdn_in_proj", transposed=True)
    wt_b = wt_in[j, n_main:n_main + hv]
    wt_a = wt_in[j, n_main + hv:]
    beta, gc = _gdn_gates(h, wt_b, wt_a, p["gdn_a_log"], p["gdn_dt_bias"], j, mp, t["tm_gates"])

    act_p, gbuf_p = _gconv_prompt(proj, p["gdn_conv_w"], j, b, tlen, conv_dim, t["tc_gconv"])
    nch = mp // CHUNK
    bcol = _by_head_group(beta[:mp], hb)
    gcol = _by_head_group(gc[:mp], hb)
    grow = jnp.transpose(gcol.reshape(hv // hb, nch, CHUNK, hb), (0, 1, 3, 2))
    grow2 = jnp.concatenate([grow, grow], axis=-1)
    u3, wq3, kd3, a4, egl3 = _gdn_prep(act_p, bcol, gcol, grow2, b, tlen, hv, key_dim, hb, t["cb"])
    o_p, s_p = _gdn_scan(u3, wq3, kd3, a4, egl3, proj, p["gdn_norm_w"], j, b, tlen, hv, conv_dim,
                         min(t["hb_scan"], hv), min(t["nc_scan"], tlen // CHUNK))

    conv_buf_t = jnp.transpose(p["state_gdn_conv"], (0, 2, 1, 3))
    act_s, gbuf_s_t = _gconv_step(proj, conv_buf_t, p["gdn_conv_w"], j, mp, bs, conv_dim,
                                  t["tc_gconv"])
    bcol_s = _by_head_group(beta[mp:], hb)
    gcol_s = _by_head_group(gc[mp:], hb)
    g3 = jnp.transpose(gcol_s, (0, 2, 1)).reshape(hv // hb, hb, bs, 1, 1)
    o_s, s_s = _gdn_step(act_s, proj, bcol_s, gcol_s, g3, p["gdn_norm_w"],
                         p["state_gdn_recurrent"], j, mp, bs, hv, key_dim, conv_dim, hb, t["bt"])

    o = jnp.concatenate([o_p, o_s], axis=0)
    d = x.shape[1]
    x = _ws_matmul(o, p["gdn_w_out"], j, (0,), [(x, _tile_spec(t["tm"], t["tn_out"]))],
                   lambda accs, e: e[0] + accs[0], d, F32, t["tm"], t["tn_out"], "gdn_out_proj")
    return x, s_p, s_s, gbuf_p, gbuf_s_t


def _layer_norm_silu(hc, lnw, lnb):
    mu = jnp.mean(hc, axis=-1, keepdims=True)
    xc = hc - mu
    y = xc * lax.rsqrt(jnp.mean(xc * xc, axis=-1, keepdims=True) + EPS)
    return _silu(y * lnw + lnb)


def _dwconv_prompt_kernel(cur_ref, prev_ref, w_ref, bdw_ref, lnw_ref, lnb_ref, o_ref, buf_ref,
                          win_ref, sh_ref, conv_ref, *, tt, kw, halo, rc, lc):
    jt = pl.program_id(1)
    d = cur_ref.shape[1]
    hist = kw - 1
    off = halo - hist
    nsh = sh_ref.shape[1]

    @pl.when(jt == 0)
    def _():
        win_ref[0:halo, :] = jnp.zeros((halo, d), F32)

    @pl.when(jt > 0)
    def _():
        win_ref[0:halo, :] = prev_ref[...]

    win_ref[halo:halo + tt, :] = cur_ref[...]
    for s in range(1, SUBLANES):
        sh_ref[s - 1] = win_ref[s:s + nsh, :]

    for r0 in range(0, tt, rc):
        for c0 in range(0, d, lc):
            lanes = slice(c0, c0 + lc)
            acc = cur_ref[r0:r0 + rc, lanes] * w_ref[hist:kw, lanes]
            for i in range(hist):
                o = off + i
                s = o % SUBLANES
                base = r0 + o - s
                if s == 0:
                    tap = win_ref[base:base + rc, lanes]
                else:
                    tap = sh_ref[s - 1, base:base + rc, lanes]
                acc = acc + tap * w_ref[i:i + 1, lanes]
            conv_ref[r0:r0 + rc, lanes] = acc + bdw_ref[:, lanes]
    o_ref[...] = _layer_norm_silu(conv_ref[...], lnw_ref[...], lnb_ref[...]).astype(o_ref.dtype)

    @pl.when(jt == pl.num_programs(1) - 1)
    def _():
        buf_ref[...] = cur_ref[tt - hist:tt, :]


def _dwconv_prompt(hc, w_dw, b_dw, ln_w, ln_b, layer, b, t, tt, halo=32):
    d = hc.shape[1]
    kw = w_dw.shape[1]
    ntb = t // tt
    hpb = tt // halo
    kern = functools.partial(_dwconv_prompt_kernel, tt=tt, kw=kw, halo=halo, rc=32, lc=512)
    vec = pl.BlockSpec((None, 1, d), lambda bi, jt: (layer, 0, 0))
    nsh = tt + halo - SUBLANES
    return pl.pallas_call(
        kern,
        grid=(b, ntb),
        in_specs=[pl.BlockSpec((tt, d), lambda bi, jt: (bi * ntb + jt, 0)),
                  pl.BlockSpec((halo, d), lambda bi, jt: (jnp.maximum((bi * ntb + jt) * hpb - 1, 0), 0)),
                  pl.BlockSpec((None, kw, d), lambda bi, jt: (layer, 0, 0)),
                  vec, vec, vec],
        out_specs=[pl.BlockSpec((tt, d), lambda bi, jt: (bi * ntb + jt, 0)),
                   pl.BlockSpec((None, kw - 1, d), lambda bi, jt: (bi, 0, 0))],
        out_shape=[jax.ShapeDtypeStruct((b * t, d), BF16),
                   jax.ShapeDtypeStruct((b, kw - 1, d), F32)],
        scratch_shapes=[pltpu.VMEM((tt + halo, d), F32),
                        pltpu.VMEM((SUBLANES - 1, nsh, d), F32),
                        pltpu.VMEM((tt, d), F32)],
        compiler_params=_cp(("parallel", "arbitrary"), 48),
        name="conf_dwconv_prompt",
    )(hc, hc, w_dw, b_dw.reshape(-1, 1, d), ln_w.reshape(-1, 1, d), ln_b.reshape(-1, 1, d))


def _dwconv_step_kernel(x_ref, buf_ref, w_ref, bdw_ref, lnw_ref, lnb_ref, o_ref, nbuf_ref, *, kw):
    x = x_ref[...]
    acc = x * w_ref[kw - 1:kw, :]
    for i in range(kw - 1):
        row = buf_ref[i]
        acc = acc + row * w_ref[i:i + 1, :]
        if i >= 1:
            nbuf_ref[i - 1] = row
    nbuf_ref[kw - 2] = x
    o_ref[...] = _layer_norm_silu(acc + bdw_ref[...], lnw_ref[...], lnb_ref[...]).astype(o_ref.dtype)


def _dwconv_step(hc, conv_buf_t, w_dw, b_dw, ln_w, ln_b, layer, row0, bs, bt):
    d = hc.shape[1]
    kw = w_dw.shape[1]
    rb = row0 // bt
    kern = functools.partial(_dwconv_step_kernel, kw=kw)
    vec = pl.BlockSpec((None, 1, d), lambda i: (layer, 0, 0))
    return pl.pallas_call(
        kern,
        grid=(bs // bt,),
        in_specs=[pl.BlockSpec((bt, d), lambda i: (rb + i, 0)),
                  pl.BlockSpec((None, kw - 1, bt, d), lambda i: (layer, 0, i, 0)),
                  pl.BlockSpec((None, kw, d), lambda i: (layer, 0, 0)),
                  vec, vec, vec],
        out_specs=[pl.BlockSpec((bt, d), lambda i: (i, 0)),
                   pl.BlockSpec((kw - 1, bt, d), lambda i: (0, i, 0))],
        out_shape=[jax.ShapeDtypeStruct((bs, d), BF16),
                   jax.ShapeDtypeStruct((kw - 1, bs, d), F32)],
        compiler_params=_cp(("parallel",), 40),
        name="conf_dwconv_step",
    )(hc, conv_buf_t, w_dw, b_dw.reshape(-1, 1, d), ln_w.reshape(-1, 1, d), ln_b.reshape(-1, 1, d))


def _glu_epilogue(accs, e):
    return (accs[0] + e[0]) * jax.nn.sigmoid(accs[1] + e[1])


def _conf_mixer(x, h, p, j, dims, t):
    b, tlen, bs = dims["b"], dims["t"], dims["bs"]
    mp = b * tlen
    d = x.shape[1]
    cd = p["conf_w_dw"].shape[2]
    tn = t["tn_pw1"]
    b1 = p["conf_b_pw1"].reshape(-1, 1, 2 * cd)
    hc = _ws_matmul(h, p["conf_w_pw1"], j, (0, cd // tn),
                    [(b1, _row_spec(tn, j)), (b1, _row_spec(tn, j, cd // tn))],
                    _glu_epilogue, cd, F32, t["tm"], tn, "conf_pw1")
    y_p, cbuf_p = _dwconv_prompt(hc, p["conf_w_dw"], p["conf_b_dw"], p["conf_ln_w"], p["conf_ln_b"],
                                 j, b, tlen, t["tt_dw"])
    conv_buf_t = jnp.transpose(p["state_conformer_conv"], (0, 2, 1, 3))
    y_s, cbuf_s_t = _dwconv_step(hc, conv_buf_t, p["conf_w_dw"], p["conf_b_dw"],
                                 p["conf_ln_w"], p["conf_ln_b"], j, mp, bs, t["bt"])
    y = jnp.concatenate([y_p, y_s], axis=0)
    b2 = p["conf_b_pw2"].reshape(-1, 1, d)
    tn2 = t["tn_out"]
    x = _ws_matmul(y, p["conf_w_pw2"], j, (0,),
                   [(b2, _row_spec(tn2, j)), (x, _tile_spec(t["tm"], tn2))],
                   lambda accs, e: e[1] + (accs[0] + e[0]), d, F32, t["tm"], tn2, "conf_pw2")
    return x, cbuf_p, cbuf_s_t


def _tiles(m, d, f):
    if m == 8320 and d == 4096 and f == 11008:
        return dict(tm=1040, tm_norm=520, tn_up=256, tm_down=520, tn_down=512, tkw_down=1376,
                    tn_in=512, tn_out=512, tn_pw1=256, tc_gconv=256, tt_dw=128, tm_gates=640,
                    hb=4, bt=8, cb=4, hb_scan=8, nc_scan=8)
    return dict(tm=m, tm_norm=m, tn_up=256, tm_down=m, tn_down=256, tkw_down=256,
                tn_in=256, tn_out=256, tn_pw1=256, tc_gconv=256, tt_dw=64, tm_gates=m,
                hb=4, bt=8, cb=2, hb_scan=8, nc_scan=2)


def _trunk(p, depth):
    xp, xs = p["x_prompt"], p["x_sample"]
    b, tlen, d = xp.shape
    bs = xs.shape[0]
    mp = b * tlen
    dims = dict(b=b, t=tlen, bs=bs)
    x = jnp.concatenate([xp.reshape(mp, d), xs.reshape(bs, d)], axis=0)
    f = p["ffn_pre_w_down"].shape[1]
    t = _tiles(x.shape[0], d, f)

    s_p, s_s, gb_p, gb_s, cb_p, cb_s = [], [], [], [], [], []
    for i in range(depth):
        x = _ffn(x, p["ffn_norm_pre"], p["ffn_pre_w_gate_up"], p["ffn_pre_w_down"], i, t)
        h = _rms_norm(x, p["mixer_norm"], i, BF16, t["tm_norm"])
        j = i // 2
        if i % 2 == 0:
            x, sp, ss, gp, gs = _gdn_mixer(x, h, p, j, dims, t)
            s_p.append(sp); s_s.append(ss); gb_p.append(gp); gb_s.append(gs)
        else:
            x, cp, cs = _conf_mixer(x, h, p, j, dims, t)
            cb_p.append(cp); cb_s.append(cs)
        x = _ffn(x, p["ffn_norm_post"], p["ffn_post_w_gate_up"], p["ffn_post_w_down"], i, t)

    fn = p["final_norm"].reshape(1, d)
    tmf = 512 if mp % 512 == 0 else mp
    y_p = _rms_norm(x, fn, 0, F32, tmf, 0, mp)
    y_s = _rms_norm(x, fn, 0, F32, bs, mp, bs)
    gbuf_s = jnp.transpose(jnp.stack(gb_s), (0, 2, 1, 3))
    cbuf_s = jnp.transpose(jnp.stack(cb_s), (0, 2, 1, 3))
    return (y_p.reshape(b, tlen, d), y_s.reshape(bs, 1, d), jnp.stack(s_p), jnp.stack(s_s),
            jnp.stack(gb_p), gbuf_s, jnp.stack(cb_p), cbuf_s)


def kernel(x_prompt, x_sample, state_gdn_recurrent, state_gdn_conv, state_conformer_conv,
           ffn_norm_pre, ffn_pre_w_gate_up, ffn_pre_w_down, mixer_norm, ffn_norm_post,
           ffn_post_w_gate_up, ffn_post_w_down, gdn_w_in, gdn_conv_w, gdn_a_log, gdn_dt_bias,
           gdn_norm_w, gdn_w_out, conf_w_pw1, conf_b_pw1, conf_w_dw, conf_b_dw, conf_ln_w,
           conf_ln_b, conf_w_pw2, conf_b_pw2, final_norm):
    p = dict(x_prompt=x_prompt, x_sample=x_sample, state_gdn_recurrent=state_gdn_recurrent,
             state_gdn_conv=state_gdn_conv, state_conformer_conv=state_conformer_conv,
             ffn_norm_pre=ffn_norm_pre, ffn_pre_w_gate_up=ffn_pre_w_gate_up,
             ffn_pre_w_down=ffn_pre_w_down, mixer_norm=mixer_norm, ffn_norm_post=ffn_norm_post,
             ffn_post_w_gate_up=ffn_post_w_gate_up, ffn_post_w_down=ffn_post_w_down,
             gdn_w_in=gdn_w_in, gdn_conv_w=gdn_conv_w, gdn_a_log=gdn_a_log,
             gdn_dt_bias=gdn_dt_bias, gdn_norm_w=gdn_norm_w, gdn_w_out=gdn_w_out,
             conf_w_pw1=conf_w_pw1, conf_b_pw1=conf_b_pw1, conf_w_dw=conf_w_dw,
             conf_b_dw=conf_b_dw, conf_ln_w=conf_ln_w, conf_ln_b=conf_ln_b,
             conf_w_pw2=conf_w_pw2, conf_b_pw2=conf_b_pw2, final_norm=final_norm)
    return _trunk(p, ffn_norm_pre.shape[0])
```

```python
import functools

import jax
import jax.numpy as jnp
from jax import lax
from jax.experimental import pallas as pl
from jax.experimental.pallas import tpu as pltpu

F32 = jnp.float32
BF16 = jnp.bfloat16
EPS = 1e-6
HEAD = 128
CHUNK = 64
SUBLANES = 8
BF16_ROWS = 16
MIB = 1024 * 1024
V7X_VMEM_CAP = 56 * MIB
HIGHEST = lax.Precision.HIGHEST


def _cp(semantics, vmem_mib):
    return pltpu.CompilerParams(dimension_semantics=semantics,
                                vmem_limit_bytes=min(vmem_mib * MIB, V7X_VMEM_CAP))


def _dot(a, b):
    return jnp.dot(a, b, preferred_element_type=F32)


def _dot_nt(a, b):
    return lax.dot_general(a, b, (((1,), (1,)), ((), ())), preferred_element_type=F32)


def _dot_tn(a, b):
    return lax.dot_general(a, b, (((0,), (0,)), ((), ())), preferred_element_type=F32)


def _dot_hi(a, b):
    return jnp.dot(a, b, precision=HIGHEST, preferred_element_type=F32)


def _silu(x):
    return x * jax.nn.sigmoid(x)


def _rms_kernel(x_ref, w_ref, o_ref):
    x = x_ref[...]
    inv = lax.rsqrt(jnp.mean(x * x, axis=-1, keepdims=True) + EPS)
    o_ref[...] = (x * inv * w_ref[...]).astype(o_ref.dtype)


def _rms_norm(x, w_stack, layer, out_dtype, tm, row0=0, rows=None):
    m, d = x.shape
    rows = m if rows is None else rows
    w3 = w_stack.reshape(-1, 1, d)
    blk0 = row0 // tm
    return pl.pallas_call(
        _rms_kernel,
        grid=(rows // tm,),
        in_specs=[pl.BlockSpec((tm, d), lambda i: (i + blk0, 0)),
                  pl.BlockSpec((None, 1, d), lambda i: (layer, 0, 0))],
        out_specs=pl.BlockSpec((tm, d), lambda i: (i, 0)),
        out_shape=jax.ShapeDtypeStruct((rows, d), out_dtype),
        compiler_params=_cp(("parallel",), 40),
        name="rms_norm",
    )(x, w3)


def _ws_kernel(*refs, n_w, n_e, epilogue, transposed, n_sub):
    a_ref = refs[0]
    w_refs = refs[1:1 + n_w]
    e_refs = refs[1 + n_w:1 + n_w + n_e]
    o_ref = refs[1 + n_w + n_e]
    wb_refs = refs[2 + n_w + n_e:]

    @pl.when(pl.program_id(1) == 0)
    def _():
        for w_ref, wb_ref in zip(w_refs, wb_refs):
            wb_ref[...] = w_ref[...].astype(BF16)

    mm = _dot_nt if transposed else _dot
    tm = a_ref.shape[0]
    ts = -(-tm // (n_sub * BF16_ROWS)) * BF16_ROWS
    for r0 in range(0, tm, ts):
        rows = slice(r0, min(r0 + ts, tm))
        a = a_ref[rows, :]
        accs = [mm(a, wb_ref[...]) for wb_ref in wb_refs]
        extras = [e_ref[rows, :] if e_ref.shape[0] == tm else e_ref[...] for e_ref in e_refs]
        o_ref[rows, :] = epilogue(accs, extras).astype(o_ref.dtype)


def _ws_matmul(a, w_stack, layer, col_blocks, extras, epilogue, n_out, out_dtype, tm, tn,
               name, transposed=False, vmem_mib=48, n_sub=1):
    m, k = a.shape
    n_w = len(col_blocks)
    in_specs = [pl.BlockSpec((tm, k), lambda j, i: (i, 0))]
    for cb in col_blocks:
        if transposed:
            in_specs.append(pl.BlockSpec((None, tn, k), lambda j, i, cb=cb: (layer, j + cb, 0)))
        else:
            in_specs.append(pl.BlockSpec((None, k, tn), lambda j, i, cb=cb: (layer, 0, j + cb)))
    in_specs += [spec for _, spec in extras]
    kern = functools.partial(_ws_kernel, n_w=n_w, n_e=len(extras), epilogue=epilogue,
                             transposed=transposed, n_sub=n_sub)
    wb_shape = (tn, k) if transposed else (k, tn)
    return pl.pallas_call(
        kern,
        grid=(n_out // tn, m // tm),
        in_specs=in_specs,
        out_specs=pl.BlockSpec((tm, tn), lambda j, i: (i, j)),
        out_shape=jax.ShapeDtypeStruct((m, n_out), out_dtype),
        scratch_shapes=[pltpu.VMEM(wb_shape, BF16) for _ in range(n_w)],
        compiler_params=_cp(("parallel", "arbitrary"), vmem_mib),
        name=name,
    )(a, *([w_stack] * n_w), *[arr for arr, _ in extras])


def _row_spec(tn, layer, col0_blocks=0):
    return pl.BlockSpec((None, 1, tn), lambda j, i: (layer, 0, j + col0_blocks))


def _tile_spec(tm, tn):
    return pl.BlockSpec((tm, tn), lambda j, i: (i, j))


def _swiglu_epilogue(accs, _):
    g, u = accs
    return _silu(g) * u


def _down_kernel(a_ref, w_ref, x_ref, o_ref, wb_ref, *, nkw, tkw, scale):
    s = pl.program_id(1)

    @pl.when(s < nkw)
    def _():
        off = pl.multiple_of(s * tkw, 16)
        wb_ref[pl.ds(off, tkw), :] = w_ref[...].astype(BF16)

    @pl.when(s >= nkw)
    def _():
        o_ref[...] = x_ref[...] + scale * _dot(a_ref[...], wb_ref[...])


def _down_matmul(h, w_stack, layer, x, scale, tm, tn, tkw):
    m, kdim = h.shape
    n = x.shape[1]
    nkw = kdim // tkw
    row = lambda s: jnp.maximum(s - nkw, 0)
    kern = functools.partial(_down_kernel, nkw=nkw, tkw=tkw, scale=scale)
    return pl.pallas_call(
        kern,
        grid=(n // tn, nkw + m // tm),
        in_specs=[pl.BlockSpec((tm, kdim), lambda j, s: (row(s), 0)),
                  pl.BlockSpec((None, tkw, tn), lambda j, s: (layer, jnp.minimum(s, nkw - 1), j)),
                  pl.BlockSpec((tm, tn), lambda j, s: (row(s), j))],
        out_specs=pl.BlockSpec((tm, tn), lambda j, s: (row(s), j)),
        out_shape=jax.ShapeDtypeStruct((m, n), F32),
        scratch_shapes=[pltpu.VMEM((kdim, tn), BF16)],
        compiler_params=_cp(("parallel", "arbitrary"), 52),
        name="ffn_down",
    )(h, w_stack, x)


def _ffn(x, norm_w, w_gu, w_down, layer, t):
    f = w_down.shape[1]
    h = _rms_norm(x, norm_w, layer, BF16, t["tm_norm"])
    tn = t["tn_up"]
    hid = _ws_matmul(h, w_gu, layer, (0, f // tn), [], _swiglu_epilogue, f, BF16, t["tm_up"], tn,
                     "ffn_up", n_sub=t["n_sub"], vmem_mib=56)
    return _down_matmul(hid, w_down, layer, x, 0.5, t["tm_down"], t["tn_down"], t["tkw_down"])


def _gates_kernel(a_ref, wb_ref, wa_ref, alog_ref, dtb_ref, beta_ref, gc_ref, *, mp, tm):
    a = a_ref[...]
    b = _dot_nt(a, wb_ref[...].astype(BF16))
    t = _dot_nt(a, wa_ref[...].astype(BF16)) + dtb_ref[...]
    beta_ref[...] = jax.nn.sigmoid(b)
    softplus = jnp.maximum(t, 0.0) + jnp.log1p(jnp.exp(-jnp.abs(t)))
    g = -jnp.exp(alog_ref[...]) * softplus
    r0 = pl.program_id(0) * tm
    row = lax.broadcasted_iota(jnp.int32, (tm, tm), 0) + r0
    col = lax.broadcasted_iota(jnp.int32, (tm, tm), 1) + r0
    chunk_of = lambda i: jnp.where(i < mp, i // CHUNK, i + mp)
    mask = jnp.where(chunk_of(row) == chunk_of(col), jnp.where(col <= row, 1.0, 0.0), 0.0)
    gc_ref[...] = _dot_hi(mask, g)


def _gdn_gates(h, wt_b, wt_a, a_log, dt_bias, layer, mp, tm):
    m, k = h.shape
    hv = wt_b.shape[0]
    vec = pl.BlockSpec((None, 1, hv), lambda i: (layer, 0, 0))
    kern = functools.partial(_gates_kernel, mp=mp, tm=tm)
    return pl.pallas_call(
        kern,
        grid=(m // tm,),
        in_specs=[pl.BlockSpec((tm, k), lambda i: (i, 0)),
                  pl.BlockSpec((hv, k), lambda i: (0, 0)),
                  pl.BlockSpec((hv, k), lambda i: (0, 0)),
                  vec, vec],
        out_specs=[pl.BlockSpec((tm, hv), lambda i: (i, 0))] * 2,
        out_shape=[jax.ShapeDtypeStruct((m, hv), F32)] * 2,
        compiler_params=_cp(("parallel",), 40),
        name="gdn_gates",
    )(h, wt_b, wt_a, a_log.reshape(-1, 1, hv), dt_bias.reshape(-1, 1, hv))


def _gconv_prompt_kernel(x_ref, w_ref, act_ref, buf_ref, pad_ref, *, t, kw):
    tc = x_ref.shape[1]
    pad_ref[0:SUBLANES, :] = jnp.zeros((SUBLANES, tc), F32)
    pad_ref[SUBLANES:SUBLANES + t, :] = x_ref[...]
    acc = x_ref[...] * w_ref[kw - 1:kw, :]
    for i in range(kw - 1):
        acc = acc + pad_ref[pl.ds(SUBLANES - (kw - 1) + i, t), :] * w_ref[i:i + 1, :]
    act_ref[...] = _silu(acc)
    buf_ref[...] = x_ref[t - (kw - 1):t, :]


def _gconv_prompt(proj, conv_w, layer, b, t, conv_dim, tc):
    kw = conv_w.shape[1]
    kern = functools.partial(_gconv_prompt_kernel, t=t, kw=kw)
    return pl.pallas_call(
        kern,
        grid=(b, conv_dim // tc),
        in_specs=[pl.BlockSpec((t, tc), lambda bi, c: (bi, c)),
                  pl.BlockSpec((None, kw, tc), lambda bi, c: (layer, 0, c))],
        out_specs=[pl.BlockSpec((t, tc), lambda bi, c: (bi, c)),
                   pl.BlockSpec((None, kw - 1, tc), lambda bi, c: (bi, 0, c))],
        out_shape=[jax.ShapeDtypeStruct((b * t, conv_dim), F32),
                   jax.ShapeDtypeStruct((b, kw - 1, conv_dim), F32)],
        scratch_shapes=[pltpu.VMEM((t + SUBLANES, tc), F32)],
        compiler_params=_cp(("parallel", "parallel"), 40),
        name="gdn_conv_prompt",
    )(proj, conv_w)


def _gconv_step_kernel(x_ref, buf_ref, w_ref, act_ref, nbuf_ref, *, kw):
    x = x_ref[...]
    acc = x * w_ref[kw - 1:kw, :]
    for i in range(kw - 1):
        row = buf_ref[i]
        acc = acc + row * w_ref[i:i + 1, :]
        if i >= 1:
            nbuf_ref[i - 1] = row
    nbuf_ref[kw - 2] = x
    act_ref[...] = _silu(acc)


def _gconv_step(proj, conv_buf_t, conv_w, layer, row0, bs, conv_dim, tc):
    kw = conv_w.shape[1]
    kern = functools.partial(_gconv_step_kernel, kw=kw)
    rb = row0 // bs
    return pl.pallas_call(
        kern,
        grid=(conv_dim // tc,),
        in_specs=[pl.BlockSpec((bs, tc), lambda c: (rb, c)),
                  pl.BlockSpec((None, kw - 1, bs, tc), lambda c: (layer, 0, 0, c)),
                  pl.BlockSpec((None, kw, tc), lambda c: (layer, 0, c))],
        out_specs=[pl.BlockSpec((bs, tc), lambda c: (0, c)),
                   pl.BlockSpec((kw - 1, bs, tc), lambda c: (0, 0, c))],
        out_shape=[jax.ShapeDtypeStruct((bs, conv_dim), F32),
                   jax.ShapeDtypeStruct((kw - 1, bs, conv_dim), F32)],
        compiler_params=_cp(("parallel",), 40),
        name="gdn_conv_step",
    )(proj, conv_buf_t, conv_w)


def _l2norm(x):
    return x * lax.rsqrt(jnp.sum(x * x, axis=-1, keepdims=True) + EPS)


def _gated_out(o, z, nw):
    o = o * lax.rsqrt(jnp.mean(o * o, axis=-1, keepdims=True) + EPS) * nw
    return o * _silu(z)


def _gdn_prep_kernel(q_ref, k_ref, v_ref, bcol_ref, gcol_ref, grow_ref,
                     u_ref, wq_ref, kd_ref, a_ref, egl_ref, *, hb, cb):
    c = CHUNK
    row = lax.broadcasted_iota(jnp.int32, (c, 2 * c), 0)
    col2 = lax.broadcasted_iota(jnp.int32, (c, 2 * c), 1)
    left = col2 < c
    col = jnp.where(left, col2, col2 - c)
    tril2 = row >= col
    strict2 = row > col
    eye_left = jnp.where(row == col2, 1.0, 0.0)
    zeros_top = jnp.zeros((c, 2 * c), BF16)
    tril = (lax.broadcasted_iota(jnp.int32, (c, c), 0) >= lax.broadcasted_iota(jnp.int32, (c, c), 1))

    qk_units = [(ci, p) for ci in range(cb) for p in range(hb // 2)]
    units = [(ci, hh) for ci in range(cb) for hh in range(hb)]
    rows_of = lambda ci: slice(ci * c, (ci + 1) * c)
    lanes_of = lambda i: slice(i * HEAD, (i + 1) * HEAD)

    qn, kn, kq = {}, {}, {}
    for ci, p in qk_units:
        qn[ci, p] = _l2norm(q_ref[rows_of(ci), lanes_of(p)]) * (HEAD ** -0.5)
        kn[ci, p] = _l2norm(k_ref[rows_of(ci), lanes_of(p)])
    for ci, p in qk_units:
        kn16 = kn[ci, p].astype(BF16)
        kq[ci, p] = _dot_nt(jnp.concatenate([kn16, qn[ci, p].astype(BF16)], axis=0),
                            jnp.concatenate([kn16, kn16], axis=0))

    gc, beta, decay2, pair = {}, {}, {}, {}
    for ci, hh in units:
        gc[ci, hh] = gcol_ref[rows_of(ci), hh:hh + 1]
        beta[ci, hh] = bcol_ref[rows_of(ci), hh:hh + 1]
        gr2 = grow_ref[ci, hh:hh + 1, :]
        decay2[ci, hh] = jnp.where(tril2, jnp.exp(jnp.where(tril2, gc[ci, hh] - gr2, 0.0)), 0.0)
        kk2 = kq[ci, hh // 2][:c]
        neg_low2 = jnp.where(strict2, -(kk2 * beta[ci, hh]) * decay2[ci, hh], 0.0)
        pair[ci, hh] = jnp.where(left, eye_left, neg_low2)

    power = 1
    while power < c:
        prod = {}
        for key in units:
            pair16 = pair[key].astype(BF16)
            prod[key] = _dot(pair16, jnp.concatenate([zeros_top, pair16], axis=0))
        for key in units:
            pair[key] = prod[key] + jnp.where(left, pair[key], 0.0)
        power *= 2

    egc, uw = {}, {}
    for ci, hh in units:
        egc[ci, hh] = jnp.exp(gc[ci, hh])
        rhs = jnp.concatenate([v_ref[rows_of(ci), lanes_of(hh)] * beta[ci, hh],
                               kn[ci, hh // 2] * (beta[ci, hh] * egc[ci, hh])], axis=1)
        rhs16 = rhs.astype(BF16)
        tinv16 = pair[ci, hh].astype(BF16)
        uw[ci, hh] = _dot(tinv16, jnp.concatenate([rhs16, jnp.zeros_like(rhs16)], axis=0))

    for ci, hh in units:
        lanes = lanes_of(hh)
        u_ref[ci, :, lanes] = uw[ci, hh][:, :HEAD]
        wq_ref[ci, 0:c, lanes] = uw[ci, hh][:, HEAD:].astype(BF16)
        wq_ref[ci, c:2 * c, lanes] = (qn[ci, hh // 2] * egc[ci, hh]).astype(BF16)
        qk = kq[ci, hh // 2][c:, :c]
        a_ref[ci, hh] = jnp.where(tril, qk * decay2[ci, hh][:, :c], 0.0).astype(BF16)
        gl = gc[ci, hh][c - 1:c, :]
        kd_ref[ci, :, lanes] = (kn[ci, hh // 2] * jnp.exp(gl - gc[ci, hh])).astype(BF16)
        egl_ref[ci, :, lanes] = jnp.broadcast_to(jnp.exp(gl), (SUBLANES, HEAD))


def _gdn_prep(act, bcol, gcol, grow2, b, t, hv, key_dim, hb, cb):
    nch = b * t // CHUNK
    hg = hv // hb
    wqk = (hb // 2) * HEAD
    wv = hb * HEAD
    c = CHUNK
    kern = functools.partial(_gdn_prep_kernel, hb=hb, cb=cb)
    return pl.pallas_call(
        kern,
        grid=(nch // cb, hg),
        in_specs=[pl.BlockSpec((cb * c, wqk), lambda i, g: (i, g)),
                  pl.BlockSpec((cb * c, wqk), lambda i, g: (i, key_dim // wqk + g)),
                  pl.BlockSpec((cb * c, wv), lambda i, g: (i, 2 * key_dim // wv + g)),
                  pl.BlockSpec((None, cb * c, hb), lambda i, g: (g, i, 0)),
                  pl.BlockSpec((None, cb * c, hb), lambda i, g: (g, i, 0)),
                  pl.BlockSpec((None, cb, hb, 2 * c), lambda i, g: (g, i, 0, 0))],
        out_specs=[pl.BlockSpec((cb, c, wv), lambda i, g: (i, 0, g)),
                   pl.BlockSpec((cb, 2 * c, wv), lambda i, g: (i, 0, g)),
                   pl.BlockSpec((cb, c, wv), lambda i, g: (i, 0, g)),
                   pl.BlockSpec((cb, hb, c, c), lambda i, g: (i, g, 0, 0)),
                   pl.BlockSpec((cb, SUBLANES, wv), lambda i, g: (i, 0, g))],
        out_shape=[jax.ShapeDtypeStruct((nch, c, hv * HEAD), F32),
                   jax.ShapeDtypeStruct((nch, 2 * c, hv * HEAD), BF16),
                   jax.ShapeDtypeStruct((nch, c, hv * HEAD), BF16),
                   jax.ShapeDtypeStruct((nch, hv, c, c), BF16),
                   jax.ShapeDtypeStruct((nch, SUBLANES, hv * HEAD), F32)],
        compiler_params=_cp(("parallel", "parallel"), 40),
        name="gdn_prep",
    )(act, act, act, bcol, gcol, grow2)


def _gdn_scan_kernel(u_ref, wq_ref, kd_ref, a_ref, egl_ref, z_ref, nw_ref, o_ref, s_ref, *, hb, nc):
    c = CHUNK

    @pl.when(pl.program_id(2) == 0)
    def _():
        s_ref[...] = jnp.zeros_like(s_ref)

    nw = nw_ref[...]

    def body(n, carry):
        r0 = pl.multiple_of(n * c, c)
        heads = range(hb)
        lanes = [slice(hh * HEAD, (hh + 1) * HEAD) for hh in heads]
        s = [s_ref[hh] for hh in heads]
        res = [_dot(wq_ref[n, :, lanes[hh]], s[hh].astype(BF16)) for hh in heads]
        v16 = [(u_ref[n, :, lanes[hh]] - res[hh][:c]).astype(BF16) for hh in heads]
        upd = [_dot_tn(kd_ref[n, :, lanes[hh]], v16[hh]) for hh in heads]
        o_intra = [_dot(a_ref[n, hh], v16[hh]) for hh in heads]
        for hh in heads:
            s_ref[hh] = s[hh] * egl_ref[n, 0:1, lanes[hh]] + upd[hh]
        for hh in heads:
            o = res[hh][c:] + o_intra[hh]
            o_ref[pl.ds(r0, c), lanes[hh]] = _gated_out(o, z_ref[pl.ds(r0, c), lanes[hh]], nw).astype(o_ref.dtype)
        return carry

    lax.fori_loop(0, nc, body, 0)


def _gdn_scan(u3, wq3, kd3, a4, egl3, proj, norm_w, layer, b, t, hv, conv_dim, hb, nc):
    ntb = (t // CHUNK) // nc
    hg = hv // hb
    wv = hb * HEAD
    c = CHUNK
    kern = functools.partial(_gdn_scan_kernel, hb=hb, nc=nc)
    blk = lambda bi, g, tb: bi * ntb + tb
    return pl.pallas_call(
        kern,
        grid=(b, hg, ntb),
        in_specs=[pl.BlockSpec((nc, c, wv), lambda bi, g, tb: (blk(bi, g, tb), 0, g)),
                  pl.BlockSpec((nc, 2 * c, wv), lambda bi, g, tb: (blk(bi, g, tb), 0, g)),
                  pl.BlockSpec((nc, c, wv), lambda bi, g, tb: (blk(bi, g, tb), 0, g)),
                  pl.BlockSpec((nc, hb, c, c), lambda bi, g, tb: (blk(bi, g, tb), g, 0, 0)),
                  pl.BlockSpec((nc, SUBLANES, wv), lambda bi, g, tb: (blk(bi, g, tb), 0, g)),
                  pl.BlockSpec((nc * c, wv), lambda bi, g, tb: (blk(bi, g, tb), conv_dim // wv + g)),
                  pl.BlockSpec((None, 1, HEAD), lambda bi, g, tb: (layer, 0, 0))],
        out_specs=[pl.BlockSpec((nc * c, wv), lambda bi, g, tb: (blk(bi, g, tb), g)),
                   pl.BlockSpec((None, hb, HEAD, HEAD), lambda bi, g, tb: (bi, g, 0, 0))],
        out_shape=[jax.ShapeDtypeStruct((b * t, hv * HEAD), BF16),
                   jax.ShapeDtypeStruct((b, hv, HEAD, HEAD), F32)],
        compiler_params=_cp(("parallel", "parallel", "arbitrary"), 40),
        name="gdn_scan",
    )(u3, wq3, kd3, a4, egl3, proj, norm_w.reshape(-1, 1, HEAD))


def _gdn_step_kernel(q_ref, k_ref, v_ref, z_ref, bcol_ref, gcol_ref, g3_ref, nw_ref, s0_ref,
                     o_ref, s_ref, *, hb, bt):
    rows = lax.broadcasted_iota(jnp.int32, (bt, bt * HEAD), 0)
    cols = lax.broadcasted_iota(jnp.int32, (bt, bt * HEAD), 1)
    bd = (cols // HEAD) == rows
    bd2 = jnp.concatenate([bd, bd], axis=0)
    nw = nw_ref[...]
    beta_all = bcol_ref[...]
    g_all = gcol_ref[...]

    def block_diag(x, mask):
        return jnp.where(mask, jnp.concatenate([x] * bt, axis=1), 0.0).astype(BF16)

    for p in range(hb // 2):
        lanes_qk = slice(p * HEAD, (p + 1) * HEAD)
        qn = _l2norm(q_ref[:, lanes_qk]) * (HEAD ** -0.5)
        kn = _l2norm(k_ref[:, lanes_qk])
        qk = jnp.sum(qn * kn, axis=-1, keepdims=True)
        for r in range(2):
            hh = 2 * p + r
            lanes = slice(hh * HEAD, (hh + 1) * HEAD)
            beta = beta_all[:, hh:hh + 1]
            eg = jnp.exp(g_all[:, hh:hh + 1])
            u = v_ref[:, lanes] * beta
            w = kn * beta * eg
            s3 = s0_ref[:, hh]
            s2 = s3.reshape(bt * HEAD, HEAD).astype(BF16)
            res = _dot(block_diag(jnp.concatenate([w, qn * eg], axis=0), bd2), s2)
            v_new = u - res[:bt]
            o = res[bt:] + qk * v_new
            upd = _dot_tn(block_diag(kn, bd), v_new.astype(BF16))
            s_ref[:, hh] = s3 * jnp.exp(g3_ref[hh]) + upd.reshape(bt, HEAD, HEAD)
            o_ref[:, lanes] = _gated_out(o, z_ref[:, lanes], nw).astype(o_ref.dtype)


def _gdn_step(act, proj, bcol, gcol, g3, norm_w, state, layer, row0, bs, hv, key_dim, conv_dim,
              hb, bt):
    hg = hv // hb
    wqk = (hb // 2) * HEAD
    wv = hb * HEAD
    rb = row0 // bt
    kern = functools.partial(_gdn_step_kernel, hb=hb, bt=bt)
    return pl.pallas_call(
        kern,
        grid=(bs // bt, hg),
        in_specs=[pl.BlockSpec((bt, wqk), lambda i, g: (i, g)),
                  pl.BlockSpec((bt, wqk), lambda i, g: (i, key_dim // wqk + g)),
                  pl.BlockSpec((bt, wv), lambda i, g: (i, 2 * key_dim // wv + g)),
                  pl.BlockSpec((bt, wv), lambda i, g: (rb + i, conv_dim // wv + g)),
                  pl.BlockSpec((None, bt, hb), lambda i, g: (g, i, 0)),
                  pl.BlockSpec((None, bt, hb), lambda i, g: (g, i, 0)),
                  pl.BlockSpec((None, hb, bt, 1, 1), lambda i, g: (g, 0, i, 0, 0)),
                  pl.BlockSpec((None, 1, HEAD), lambda i, g: (layer, 0, 0)),
                  pl.BlockSpec((None, bt, hb, HEAD, HEAD), lambda i, g: (layer, i, g, 0, 0))],
        out_specs=[pl.BlockSpec((bt, wv), lambda i, g: (i, g)),
                   pl.BlockSpec((bt, hb, HEAD, HEAD), lambda i, g: (i, g, 0, 0))],
        out_shape=[jax.ShapeDtypeStruct((bs, hv * HEAD), BF16),
                   jax.ShapeDtypeStruct((bs, hv, HEAD, HEAD), F32)],
        compiler_params=_cp(("parallel", "parallel"), 40),
        name="gdn_step",
    )(act, act, act, proj, bcol, gcol, g3, norm_w.reshape(-1, 1, HEAD), state)


def _by_head_group(x, hb):
    r, hv = x.shape
    return jnp.transpose(x.reshape(r, hv // hb, hb), (1, 0, 2))


def _gdn_mixer(x, h, p, j, dims, t):
    b, tlen, bs = dims["b"], dims["t"], dims["bs"]
    mp = b * tlen
    hv = p["gdn_a_log"].shape[1]
    conv_dim = p["gdn_conv_w"].shape[2]
    val_dim = hv * HEAD
    key_dim = (conv_dim - val_dim) // 2
    n_main = conv_dim + val_dim
    hb = t["hb"]
    wt_in = jnp.transpose(p["gdn_w_in"], (0, 2, 1))

    proj = _ws_matmul(h, wt_in, j, (0,), [], lambda accs, _: accs[0], n_main, F32, t["tm"],
                      t["tn_in"], "gdn_in_proj", transposed=True, n_sub=t["n_sub"])
    wt_b = wt_in[j, n_main:n_main + hv]
    wt_a = wt_in[j, n_main + hv:]
    beta, gc = _gdn_gates(h, wt_b, wt_a, p["gdn_a_log"], p["gdn_dt_bias"], j, mp, t["tm_gates"])

    act_p, gbuf_p = _gconv_prompt(proj, p["gdn_conv_w"], j, b, tlen, conv_dim, t["tc_gconv"])
    nch = mp // CHUNK
    bcol = _by_head_group(beta[:mp], hb)
    gcol = _by_head_group(gc[:mp], hb)
    grow = jnp.transpose(gcol.reshape(hv // hb, nch, CHUNK, hb), (0, 1, 3, 2))
    grow2 = jnp.concatenate([grow, grow], axis=-1)
    u3, wq3, kd3, a4, egl3 = _gdn_prep(act_p, bcol, gcol, grow2, b, tlen, hv, key_dim, hb, t["cb"])
    o_p, s_p = _gdn_scan(u3, wq3, kd3, a4, egl3, proj, p["gdn_norm_w"], j, b, tlen, hv, conv_dim,
                         min(t["hb_scan"], hv), min(t["nc_scan"], tlen // CHUNK))

    conv_buf_t = jnp.transpose(p["state_gdn_conv"], (0, 2, 1, 3))
    act_s, gbuf_s_t = _gconv_step(proj, conv_buf_t, p["gdn_conv_w"], j, mp, bs, conv_dim,
                                  t["tc_gconv"])
    bcol_s = _by_head_group(beta[mp:], hb)
    gcol_s = _by_head_group(gc[mp:], hb)
    g3 = jnp.transpose(gcol_s, (0, 2, 1)).reshape(hv // hb, hb, bs, 1, 1)
    o_s, s_s = _gdn_step(act_s, proj, bcol_s, gcol_s, g3, p["gdn_norm_w"],
                         p["state_gdn_recurrent"], j, mp, bs, hv, key_dim, conv_dim, hb, t["bt"])

    o = jnp.concatenate([o_p, o_s], axis=0)
    d = x.shape[1]
    x = _ws_matmul(o, p["gdn_w_out"], j, (0,), [(x, _tile_spec(t["tm"], t["tn_out"]))],
                   lambda accs, e: e[0] + accs[0], d, F32, t["tm"], t["tn_out"], "gdn_out_proj",
                   n_sub=t["n_sub"])
    return x, s_p, s_s, gbuf_p, gbuf_s_t


def _layer_norm_silu(hc, lnw, lnb):
    mu = jnp.mean(hc, axis=-1, keepdims=True)
    xc = hc - mu
    y = xc * lax.rsqrt(jnp.mean(xc * xc, axis=-1, keepdims=True) + EPS)
    return _silu(y * lnw + lnb)


def _dwconv_prompt_kernel(cur_ref, prev_ref, w_ref, bdw_ref, lnw_ref, lnb_ref, o_ref, buf_ref,
                          win_ref, sh_ref, conv_ref, *, tt, kw, halo, rc, lc):
    jt = pl.program_id(1)
    d = cur_ref.shape[1]
    hist = kw - 1
    off = halo - hist
    nsh = sh_ref.shape[1]

    @pl.when(jt == 0)
    def _():
        win_ref[0:halo, :] = jnp.zeros((halo, d), F32)

    @pl.when(jt > 0)
    def _():
        win_ref[0:halo, :] = prev_ref[...]

    win_ref[halo:halo + tt, :] = cur_ref[...]
    for s in range(1, SUBLANES):
        sh_ref[s - 1] = win_ref[s:s + nsh, :]

    for r0 in range(0, tt, rc):
        for c0 in range(0, d, lc):
            lanes = slice(c0, c0 + lc)
            acc = cur_ref[r0:r0 + rc, lanes] * w_ref[hist:kw, lanes]
            for i in range(hist):
                o = off + i
                s = o % SUBLANES
                base = r0 + o - s
                if s == 0:
                    tap = win_ref[base:base + rc, lanes]
                else:
                    tap = sh_ref[s - 1, base:base + rc, lanes]
                acc = acc + tap * w_ref[i:i + 1, lanes]
            conv_ref[r0:r0 + rc, lanes] = acc + bdw_ref[:, lanes]
    o_ref[...] = _layer_norm_silu(conv_ref[...], lnw_ref[...], lnb_ref[...]).astype(o_ref.dtype)

    @pl.when(jt == pl.num_programs(1) - 1)
    def _():
        buf_ref[...] = cur_ref[tt - hist:tt, :]


def _dwconv_prompt(hc, w_dw, b_dw, ln_w, ln_b, layer, b, t, tt, halo=32):
    d = hc.shape[1]
    kw = w_dw.shape[1]
    ntb = t // tt
    hpb = tt // halo
    kern = functools.partial(_dwconv_prompt_kernel, tt=tt, kw=kw, halo=halo, rc=32, lc=512)
    vec = pl.BlockSpec((None, 1, d), lambda bi, jt: (layer, 0, 0))
    nsh = tt + halo - SUBLANES
    return pl.pallas_call(
        kern,
        grid=(b, ntb),
        in_specs=[pl.BlockSpec((tt, d), lambda bi, jt: (bi * ntb + jt, 0)),
                  pl.BlockSpec((halo, d), lambda bi, jt: (jnp.maximum((bi * ntb + jt) * hpb - 1, 0), 0)),
                  pl.BlockSpec((None, kw, d), lambda bi, jt: (layer, 0, 0)),
                  vec, vec, vec],
        out_specs=[pl.BlockSpec((tt, d), lambda bi, jt: (bi * ntb + jt, 0)),
                   pl.BlockSpec((None, kw - 1, d), lambda bi, jt: (bi, 0, 0))],
        out_shape=[jax.ShapeDtypeStruct((b * t, d), BF16),
                   jax.ShapeDtypeStruct((b, kw - 1, d), F32)],
        scratch_shapes=[pltpu.VMEM((tt + halo, d), F32),
                        pltpu.VMEM((SUBLANES - 1, nsh, d), F32),
                        pltpu.VMEM((tt, d), F32)],
        compiler_params=_cp(("parallel", "arbitrary"), 48),
        name="conf_dwconv_prompt",
    )(hc, hc, w_dw, b_dw.reshape(-1, 1, d), ln_w.reshape(-1, 1, d), ln_b.reshape(-1, 1, d))


def _dwconv_step_kernel(x_ref, buf_ref, w_ref, bdw_ref, lnw_ref, lnb_ref, o_ref, nbuf_ref, *, kw):
    x = x_ref[...]
    acc = x * w_ref[kw - 1:kw, :]
    for i in range(kw - 1):
        row = buf_ref[i]
        acc = acc + row * w_ref[i:i + 1, :]
        if i >= 1:
            nbuf_ref[i - 1] = row
    nbuf_ref[kw - 2] = x
    o_ref[...] = _layer_norm_silu(acc + bdw_ref[...], lnw_ref[...], lnb_ref[...]).astype(o_ref.dtype)


def _dwconv_step(hc, conv_buf_t, w_dw, b_dw, ln_w, ln_b, layer, row0, bs, bt):
    d = hc.shape[1]
    kw = w_dw.shape[1]
    rb = row0 // bt
    kern = functools.partial(_dwconv_step_kernel, kw=kw)
    vec = pl.BlockSpec((None, 1, d), lambda i: (layer, 0, 0))
    return pl.pallas_call(
        kern,
        grid=(bs // bt,),
        in_specs=[pl.BlockSpec((bt, d), lambda i: (rb + i, 0)),
                  pl.BlockSpec((None, kw - 1, bt, d), lambda i: (layer, 0, i, 0)),
                  pl.BlockSpec((None, kw, d), lambda i: (layer, 0, 0)),
                  vec, vec, vec],
        out_specs=[pl.BlockSpec((bt, d), lambda i: (i, 0)),
                   pl.BlockSpec((kw - 1, bt, d), lambda i: (0, i, 0))],
        out_shape=[jax.ShapeDtypeStruct((bs, d), BF16),
                   jax.ShapeDtypeStruct((kw - 1, bs, d), F32)],
        compiler_params=_cp(("parallel",), 40),
        name="conf_dwconv_step",
    )(hc, conv_buf_t, w_dw, b_dw.reshape(-1, 1, d), ln_w.reshape(-1, 1, d), ln_b.reshape(-1, 1, d))


def _glu_epilogue(accs, e):
    return (accs[0] + e[0]) * jax.nn.sigmoid(accs[1] + e[1])


def _conf_mixer(x, h, p, j, dims, t):
    b, tlen, bs = dims["b"], dims["t"], dims["bs"]
    mp = b * tlen
    d = x.shape[1]
    cd = p["conf_w_dw"].shape[2]
    tn = t["tn_pw1"]
    b1 = p["conf_b_pw1"].reshape(-1, 1, 2 * cd)
    hc = _ws_matmul(h, p["conf_w_pw1"], j, (0, cd // tn),
                    [(b1, _row_spec(tn, j)), (b1, _row_spec(tn, j, cd // tn))],
                    _glu_epilogue, cd, F32, t["tm_up"], tn, "conf_pw1", n_sub=t["n_sub"],
                    vmem_mib=56)
    y_p, cbuf_p = _dwconv_prompt(hc, p["conf_w_dw"], p["conf_b_dw"], p["conf_ln_w"], p["conf_ln_b"],
                                 j, b, tlen, t["tt_dw"])
    conv_buf_t = jnp.transpose(p["state_conformer_conv"], (0, 2, 1, 3))
    y_s, cbuf_s_t = _dwconv_step(hc, conv_buf_t, p["conf_w_dw"], p["conf_b_dw"],
                                 p["conf_ln_w"], p["conf_ln_b"], j, mp, bs, t["bt"])
    y = jnp.concatenate([y_p, y_s], axis=0)
    b2 = p["conf_b_pw2"].reshape(-1, 1, d)
    tn2 = t["tn_out"]
    x = _ws_matmul(y, p["conf_w_pw2"], j, (0,),
                   [(b2, _row_spec(tn2, j)), (x, _tile_spec(t["tm"], tn2))],
                   lambda accs, e: e[1] + (accs[0] + e[0]), d, F32, t["tm"], tn2, "conf_pw2",
                   n_sub=t["n_sub"])
    return x, cbuf_p, cbuf_s_t


def _tiles(m, d, f):
    if m == 8320 and d == 4096 and f == 11008:
        return dict(tm=1040, tm_up=1664, tm_norm=520, tn_up=256, tm_down=520, tn_down=512, tkw_down=1376,
                    tn_in=512, tn_out=512, tn_pw1=256, tc_gconv=256, tt_dw=128, tm_gates=640,
                    hb=4, bt=8, cb=4, hb_scan=8, nc_scan=8, n_sub=2)
    return dict(tm=m, tm_up=m, tm_norm=m, tn_up=256, tm_down=m, tn_down=256, tkw_down=256,
                tn_in=256, tn_out=256, tn_pw1=256, tc_gconv=256, tt_dw=64, tm_gates=m,
                hb=4, bt=8, cb=2, hb_scan=8, nc_scan=2, n_sub=2)


def _trunk(p, depth):
    xp, xs = p["x_prompt"], p["x_sample"]
    b, tlen, d = xp.shape
    bs = xs.shape[0]
    mp = b * tlen
    dims = dict(b=b, t=tlen, bs=bs)
    x = jnp.concatenate([xp.reshape(mp, d), xs.reshape(bs, d)], axis=0)
    f = p["ffn_pre_w_down"].shape[1]
    t = _tiles(x.shape[0], d, f)

    s_p, s_s, gb_p, gb_s, cb_p, cb_s = [], [], [], [], [], []
    for i in range(depth):
        x = _ffn(x, p["ffn_norm_pre"], p["ffn_pre_w_gate_up"], p["ffn_pre_w_down"], i, t)
        h = _rms_norm(x, p["mixer_norm"], i, BF16, t["tm_norm"])
        j = i // 2
        if i % 2 == 0:
            x, sp, ss, gp, gs = _gdn_mixer(x, h, p, j, dims, t)
            s_p.append(sp); s_s.append(ss); gb_p.append(gp); gb_s.append(gs)
        else:
            x, cp, cs = _conf_mixer(x, h, p, j, dims, t)
            cb_p.append(cp); cb_s.append(cs)
        x = _ffn(x, p["ffn_norm_post"], p["ffn_post_w_gate_up"], p["ffn_post_w_down"], i, t)

    fn = p["final_norm"].reshape(1, d)
    tmf = 512 if mp % 512 == 0 else mp
    y_p = _rms_norm(x, fn, 0, F32, tmf, 0, mp)
    y_s = _rms_norm(x, fn, 0, F32, bs, mp, bs)
    gbuf_s = jnp.transpose(jnp.stack(gb_s), (0, 2, 1, 3))
    cbuf_s = jnp.transpose(jnp.stack(cb_s), (0, 2, 1, 3))
    return (y_p.reshape(b, tlen, d), y_s.reshape(bs, 1, d), jnp.stack(s_p), jnp.stack(s_s),
            jnp.stack(gb_p), gbuf_s, jnp.stack(cb_p), cbuf_s)


def kernel(x_prompt, x_sample, state_gdn_recurrent, state_gdn_conv, state_conformer_conv,
           ffn_norm_pre, ffn_pre_w_gate_up, ffn_pre_w_down, mixer_norm, ffn_norm_post,
           ffn_post_w_gate_up, ffn_post_w_down, gdn_w_in, gdn_conv_w, gdn_a_log, gdn_dt_bias,
           gdn_norm_w, gdn_w_out, conf_w_pw1, conf_b_pw1, conf_w_dw, conf_b_dw, conf_ln_w,
           conf_ln_b, conf_w_pw2, conf_b_pw2, final_norm):
    p = dict(x_prompt=x_prompt, x_sample=x_sample, state_gdn_recurrent=state_gdn_recurrent,
             state_gdn_conv=state_gdn_conv, state_conformer_conv=state_conformer_conv,
             ffn_norm_pre=ffn_norm_pre, ffn_pre_w_gate_up=ffn_pre_w_gate_up,
             ffn_pre_w_down=ffn_pre_w_down, mixer_norm=mixer_norm, ffn_norm_post=ffn_norm_post,
             ffn_post_w_gate_up=ffn_post_w_gate_up, ffn_post_w_down=ffn_post_w_down,
             gdn_w_in=gdn_w_in, gdn_conv_w=gdn_conv_w, gdn_a_log=gdn_a_log,
             gdn_dt_bias=gdn_dt_bias, gdn_norm_w=gdn_norm_w, gdn_w_out=gdn_w_out,
             conf_w_pw1=conf_w_pw1, conf_b_pw1=conf_b_pw1, conf_w_dw=conf_w_dw,
             conf_b_dw=conf_b_dw, conf_ln_w=conf_ln_w, conf_ln_b=conf_ln_b,
             conf_w_pw2=conf_w_pw2, conf_b_pw2=conf_b_pw2, final_norm=final_norm)
    return _trunk(p, ffn_norm_pre.shape[0])
```

```python
import functools

import jax
import jax.numpy as jnp
from jax import lax
from jax.experimental import pallas as pl
from jax.experimental.pallas import tpu as pltpu

F32 = jnp.float32
BF16 = jnp.bfloat16
EPS = 1e-6
HEAD = 128
CHUNK = 64
SUBLANES = 8
BF16_ROWS = 16
MIB = 1024 * 1024
V7X_VMEM_CAP = 56 * MIB
HIGHEST = lax.Precision.HIGHEST


def _cp(semantics, vmem_mib):
    return pltpu.CompilerParams(dimension_semantics=semantics,
                                vmem_limit_bytes=min(vmem_mib * MIB, V7X_VMEM_CAP))


def _dot(a, b):
    return jnp.dot(a, b, preferred_element_type=F32)


def _dot_nt(a, b):
    return lax.dot_general(a, b, (((1,), (1,)), ((), ())), preferred_element_type=F32)


def _dot_tn(a, b):
    return lax.dot_general(a, b, (((0,), (0,)), ((), ())), preferred_element_type=F32)


def _dot_hi(a, b):
    return jnp.dot(a, b, precision=HIGHEST, preferred_element_type=F32)


def _silu(x):
    return x * jax.nn.sigmoid(x)


def _rms_kernel(x_ref, w_ref, o_ref):
    x = x_ref[...]
    inv = lax.rsqrt(jnp.mean(x * x, axis=-1, keepdims=True) + EPS)
    o_ref[...] = (x * inv * w_ref[...]).astype(o_ref.dtype)


def _rms_norm(x, w_stack, layer, out_dtype, tm, row0=0, rows=None):
    m, d = x.shape
    rows = m if rows is None else rows
    w3 = w_stack.reshape(-1, 1, d)
    blk0 = row0 // tm
    return pl.pallas_call(
        _rms_kernel,
        grid=(rows // tm,),
        in_specs=[pl.BlockSpec((tm, d), lambda i: (i + blk0, 0)),
                  pl.BlockSpec((None, 1, d), lambda i: (layer, 0, 0))],
        out_specs=pl.BlockSpec((tm, d), lambda i: (i, 0)),
        out_shape=jax.ShapeDtypeStruct((rows, d), out_dtype),
        compiler_params=_cp(("parallel",), 40),
        name="rms_norm",
    )(x, w3)


def _ws_kernel(*refs, n_w, n_e, epilogue, transposed, n_sub):
    a_ref = refs[0]
    w_refs = refs[1:1 + n_w]
    e_refs = refs[1 + n_w:1 + n_w + n_e]
    o_ref = refs[1 + n_w + n_e]
    wb_refs = refs[2 + n_w + n_e:]

    @pl.when(pl.program_id(1) == 0)
    def _():
        for w_ref, wb_ref in zip(w_refs, wb_refs):
            wb_ref[...] = w_ref[...].astype(BF16)

    mm = _dot_nt if transposed else _dot
    tm = a_ref.shape[0]
    ts = -(-tm // (n_sub * BF16_ROWS)) * BF16_ROWS
    for r0 in range(0, tm, ts):
        rows = slice(r0, min(r0 + ts, tm))
        a = a_ref[rows, :]
        accs = [mm(a, wb_ref[...]) for wb_ref in wb_refs]
        extras = [e_ref[rows, :] if e_ref.shape[0] == tm else e_ref[...] for e_ref in e_refs]
        o_ref[rows, :] = epilogue(accs, extras).astype(o_ref.dtype)


def _ws_matmul(a, w_stack, layer, col_blocks, extras, epilogue, n_out, out_dtype, tm, tn,
               name, transposed=False, vmem_mib=48, n_sub=1):
    m, k = a.shape
    n_w = len(col_blocks)
    in_specs = [pl.BlockSpec((tm, k), lambda j, i: (i, 0))]
    for cb in col_blocks:
        if transposed:
            in_specs.append(pl.BlockSpec((None, tn, k), lambda j, i, cb=cb: (layer, j + cb, 0)))
        else:
            in_specs.append(pl.BlockSpec((None, k, tn), lambda j, i, cb=cb: (layer, 0, j + cb)))
    in_specs += [spec for _, spec in extras]
    kern = functools.partial(_ws_kernel, n_w=n_w, n_e=len(extras), epilogue=epilogue,
                             transposed=transposed, n_sub=n_sub)
    wb_shape = (tn, k) if transposed else (k, tn)
    return pl.pallas_call(
        kern,
        grid=(n_out // tn, m // tm),
        in_specs=in_specs,
        out_specs=pl.BlockSpec((tm, tn), lambda j, i: (i, j)),
        out_shape=jax.ShapeDtypeStruct((m, n_out), out_dtype),
        scratch_shapes=[pltpu.VMEM(wb_shape, BF16) for _ in range(n_w)],
        compiler_params=_cp(("parallel", "arbitrary"), vmem_mib),
        name=name,
    )(a, *([w_stack] * n_w), *[arr for arr, _ in extras])


def _row_spec(tn, layer, col0_blocks=0):
    return pl.BlockSpec((None, 1, tn), lambda j, i: (layer, 0, j + col0_blocks))


def _tile_spec(tm, tn):
    return pl.BlockSpec((tm, tn), lambda j, i: (i, j))


def _swiglu_epilogue(accs, _):
    g, u = accs
    return _silu(g) * u


def _down_kernel(a_ref, w_ref, x_ref, o_ref, wb_ref, *, nkw, tkw, scale):
    s = pl.program_id(1)

    @pl.when(s < nkw)
    def _():
        off = pl.multiple_of(s * tkw, 16)
        wb_ref[pl.ds(off, tkw), :] = w_ref[...].astype(BF16)

    @pl.when(s >= nkw)
    def _():
        o_ref[...] = x_ref[...] + scale * _dot(a_ref[...], wb_ref[...])


def _down_matmul(h, w_stack, layer, x, scale, tm, tn, tkw):
    m, kdim = h.shape
    n = x.shape[1]
    nkw = kdim // tkw
    row = lambda s: jnp.maximum(s - nkw, 0)
    kern = functools.partial(_down_kernel, nkw=nkw, tkw=tkw, scale=scale)
    return pl.pallas_call(
        kern,
        grid=(n // tn, nkw + m // tm),
        in_specs=[pl.BlockSpec((tm, kdim), lambda j, s: (row(s), 0)),
                  pl.BlockSpec((None, tkw, tn), lambda j, s: (layer, jnp.minimum(s, nkw - 1), j)),
                  pl.BlockSpec((tm, tn), lambda j, s: (row(s), j))],
        out_specs=pl.BlockSpec((tm, tn), lambda j, s: (row(s), j)),
        out_shape=jax.ShapeDtypeStruct((m, n), F32),
        scratch_shapes=[pltpu.VMEM((kdim, tn), BF16)],
        compiler_params=_cp(("parallel", "arbitrary"), 52),
        name="ffn_down",
    )(h, w_stack, x)


def _ffn(x, norm_w, w_gu, w_down, layer, t):
    f = w_down.shape[1]
    h = _rms_norm(x, norm_w, layer, BF16, t["tm_norm"])
    tn = t["tn_up"]
    hid = _ws_matmul(h, w_gu, layer, (0, f // tn), [], _swiglu_epilogue, f, BF16, t["tm_up"], tn,
                     "ffn_up", n_sub=t["n_sub"], vmem_mib=56)
    return _down_matmul(hid, w_down, layer, x, 0.5, t["tm_down"], t["tn_down"], t["tkw_down"])


def _gates_kernel(a_ref, wb_ref, wa_ref, alog_ref, dtb_ref, beta_ref, gc_ref, *, mp, tm):
    a = a_ref[...]
    b = _dot_nt(a, wb_ref[...].astype(BF16))
    t = _dot_nt(a, wa_ref[...].astype(BF16)) + dtb_ref[...]
    beta_ref[...] = jax.nn.sigmoid(b)
    softplus = jnp.maximum(t, 0.0) + jnp.log1p(jnp.exp(-jnp.abs(t)))
    g = -jnp.exp(alog_ref[...]) * softplus
    r0 = pl.program_id(0) * tm
    row = lax.broadcasted_iota(jnp.int32, (tm, tm), 0) + r0
    col = lax.broadcasted_iota(jnp.int32, (tm, tm), 1) + r0
    chunk_of = lambda i: jnp.where(i < mp, i // CHUNK, i + mp)
    mask = jnp.where(chunk_of(row) == chunk_of(col), jnp.where(col <= row, 1.0, 0.0), 0.0)
    gc_ref[...] = _dot_hi(mask, g)


def _gdn_gates(h, wt_b, wt_a, a_log, dt_bias, layer, mp, tm):
    m, k = h.shape
    hv = wt_b.shape[0]
    vec = pl.BlockSpec((None, 1, hv), lambda i: (layer, 0, 0))
    kern = functools.partial(_gates_kernel, mp=mp, tm=tm)
    return pl.pallas_call(
        kern,
        grid=(m // tm,),
        in_specs=[pl.BlockSpec((tm, k), lambda i: (i, 0)),
                  pl.BlockSpec((hv, k), lambda i: (0, 0)),
                  pl.BlockSpec((hv, k), lambda i: (0, 0)),
                  vec, vec],
        out_specs=[pl.BlockSpec((tm, hv), lambda i: (i, 0))] * 2,
        out_shape=[jax.ShapeDtypeStruct((m, hv), F32)] * 2,
        compiler_params=_cp(("parallel",), 40),
        name="gdn_gates",
    )(h, wt_b, wt_a, a_log.reshape(-1, 1, hv), dt_bias.reshape(-1, 1, hv))


def _gconv_prompt_kernel(x_ref, w_ref, act_ref, buf_ref, pad_ref, *, t, kw):
    tc = x_ref.shape[1]
    pad_ref[0:SUBLANES, :] = jnp.zeros((SUBLANES, tc), F32)
    pad_ref[SUBLANES:SUBLANES + t, :] = x_ref[...]
    acc = x_ref[...] * w_ref[kw - 1:kw, :]
    for i in range(kw - 1):
        acc = acc + pad_ref[pl.ds(SUBLANES - (kw - 1) + i, t), :] * w_ref[i:i + 1, :]
    act_ref[...] = _silu(acc)
    buf_ref[...] = x_ref[t - (kw - 1):t, :]


def _gconv_prompt(proj, conv_w, layer, b, t, conv_dim, tc):
    kw = conv_w.shape[1]
    kern = functools.partial(_gconv_prompt_kernel, t=t, kw=kw)
    return pl.pallas_call(
        kern,
        grid=(b, conv_dim // tc),
        in_specs=[pl.BlockSpec((t, tc), lambda bi, c: (bi, c)),
                  pl.BlockSpec((None, kw, tc), lambda bi, c: (layer, 0, c))],
        out_specs=[pl.BlockSpec((t, tc), lambda bi, c: (bi, c)),
                   pl.BlockSpec((None, kw - 1, tc), lambda bi, c: (bi, 0, c))],
        out_shape=[jax.ShapeDtypeStruct((b * t, conv_dim), F32),
                   jax.ShapeDtypeStruct((b, kw - 1, conv_dim), F32)],
        scratch_shapes=[pltpu.VMEM((t + SUBLANES, tc), F32)],
        compiler_params=_cp(("parallel", "parallel"), 40),
        name="gdn_conv_prompt",
    )(proj, conv_w)


def _gconv_step_kernel(x_ref, buf_ref, w_ref, act_ref, nbuf_ref, *, kw):
    x = x_ref[...]
    acc = x * w_ref[kw - 1:kw, :]
    for i in range(kw - 1):
        row = buf_ref[i]
        acc = acc + row * w_ref[i:i + 1, :]
        if i >= 1:
            nbuf_ref[i - 1] = row
    nbuf_ref[kw - 2] = x
    act_ref[...] = _silu(acc)


def _gconv_step(proj, conv_buf_t, conv_w, layer, row0, bs, conv_dim, tc):
    kw = conv_w.shape[1]
    kern = functools.partial(_gconv_step_kernel, kw=kw)
    rb = row0 // bs
    return pl.pallas_call(
        kern,
        grid=(conv_dim // tc,),
        in_specs=[pl.BlockSpec((bs, tc), lambda c: (rb, c)),
                  pl.BlockSpec((None, kw - 1, bs, tc), lambda c: (layer, 0, 0, c)),
                  pl.BlockSpec((None, kw, tc), lambda c: (layer, 0, c))],
        out_specs=[pl.BlockSpec((bs, tc), lambda c: (0, c)),
                   pl.BlockSpec((kw - 1, bs, tc), lambda c: (0, 0, c))],
        out_shape=[jax.ShapeDtypeStruct((bs, conv_dim), F32),
                   jax.ShapeDtypeStruct((kw - 1, bs, conv_dim), F32)],
        compiler_params=_cp(("parallel",), 40),
        name="gdn_conv_step",
    )(proj, conv_buf_t, conv_w)


def _l2norm(x):
    return x * lax.rsqrt(jnp.sum(x * x, axis=-1, keepdims=True) + EPS)


def _gated_out(o, z, nw):
    o = o * lax.rsqrt(jnp.mean(o * o, axis=-1, keepdims=True) + EPS) * nw
    return o * _silu(z)


def _gdn_prep_kernel(q_ref, k_ref, v_ref, bcol_ref, gcol_ref, grow_ref,
                     u_ref, wq_ref, kd_ref, a_ref, egl_ref, *, hb, cb):
    c = CHUNK
    row = lax.broadcasted_iota(jnp.int32, (c, 2 * c), 0)
    col2 = lax.broadcasted_iota(jnp.int32, (c, 2 * c), 1)
    left = col2 < c
    col = jnp.where(left, col2, col2 - c)
    tril2 = row >= col
    strict2 = row > col
    eye_left = jnp.where(row == col2, 1.0, 0.0)
    zeros_top = jnp.zeros((c, 2 * c), BF16)
    tril = (lax.broadcasted_iota(jnp.int32, (c, c), 0) >= lax.broadcasted_iota(jnp.int32, (c, c), 1))

    qk_units = [(ci, p) for ci in range(cb) for p in range(hb // 2)]
    units = [(ci, hh) for ci in range(cb) for hh in range(hb)]
    rows_of = lambda ci: slice(ci * c, (ci + 1) * c)
    lanes_of = lambda i: slice(i * HEAD, (i + 1) * HEAD)

    qn, kn, kq = {}, {}, {}
    for ci, p in qk_units:
        qn[ci, p] = _l2norm(q_ref[rows_of(ci), lanes_of(p)]) * (HEAD ** -0.5)
        kn[ci, p] = _l2norm(k_ref[rows_of(ci), lanes_of(p)])
    for ci, p in qk_units:
        kn16 = kn[ci, p].astype(BF16)
        kq[ci, p] = _dot_nt(jnp.concatenate([kn16, qn[ci, p].astype(BF16)], axis=0),
                            jnp.concatenate([kn16, kn16], axis=0))

    gc, beta, decay2, pair = {}, {}, {}, {}
    for ci, hh in units:
        gc[ci, hh] = gcol_ref[rows_of(ci), hh:hh + 1]
        beta[ci, hh] = bcol_ref[rows_of(ci), hh:hh + 1]
        gr2 = grow_ref[ci, hh:hh + 1, :]
        decay2[ci, hh] = jnp.where(tril2, jnp.exp(jnp.where(tril2, gc[ci, hh] - gr2, 0.0)), 0.0)
        kk2 = kq[ci, hh // 2][:c]
        neg_low2 = jnp.where(strict2, -(kk2 * beta[ci, hh]) * decay2[ci, hh], 0.0)
        pair[ci, hh] = jnp.where(left, eye_left, neg_low2)

    power = 1
    while power < c:
        prod = {}
        for key in units:
            pair16 = pair[key].astype(BF16)
            prod[key] = _dot(pair16, jnp.concatenate([zeros_top, pair16], axis=0))
        for key in units:
            pair[key] = prod[key] + jnp.where(left, pair[key], 0.0)
        power *= 2

    egc, uw = {}, {}
    for ci, hh in units:
        egc[ci, hh] = jnp.exp(gc[ci, hh])
        rhs = jnp.concatenate([v_ref[rows_of(ci), lanes_of(hh)] * beta[ci, hh],
                               kn[ci, hh // 2] * (beta[ci, hh] * egc[ci, hh])], axis=1)
        rhs16 = rhs.astype(BF16)
        tinv16 = pair[ci, hh].astype(BF16)
        uw[ci, hh] = _dot(tinv16, jnp.concatenate([rhs16, jnp.zeros_like(rhs16)], axis=0))

    for ci, hh in units:
        lanes = lanes_of(hh)
        u_ref[ci, :, lanes] = uw[ci, hh][:, :HEAD]
        wq_ref[ci, 0:c, lanes] = uw[ci, hh][:, HEAD:].astype(BF16)
        wq_ref[ci, c:2 * c, lanes] = (qn[ci, hh // 2] * egc[ci, hh]).astype(BF16)
        qk = kq[ci, hh // 2][c:, :c]
        a_ref[ci, hh] = jnp.where(tril, qk * decay2[ci, hh][:, :c], 0.0).astype(BF16)
        gl = gc[ci, hh][c - 1:c, :]
        kd_ref[ci, :, lanes] = (kn[ci, hh // 2] * jnp.exp(gl - gc[ci, hh])).astype(BF16)
        egl_ref[ci, :, lanes] = jnp.broadcast_to(jnp.exp(gl), (SUBLANES, HEAD))


def _gdn_prep(act, bcol, gcol, grow2, b, t, hv, key_dim, hb, cb):
    nch = b * t // CHUNK
    hg = hv // hb
    wqk = (hb // 2) * HEAD
    wv = hb * HEAD
    c = CHUNK
    kern = functools.partial(_gdn_prep_kernel, hb=hb, cb=cb)
    return pl.pallas_call(
        kern,
        grid=(nch // cb, hg),
        in_specs=[pl.BlockSpec((cb * c, wqk), lambda i, g: (i, g)),
                  pl.BlockSpec((cb * c, wqk), lambda i, g: (i, key_dim // wqk + g)),
                  pl.BlockSpec((cb * c, wv), lambda i, g: (i, 2 * key_dim // wv + g)),
                  pl.BlockSpec((None, cb * c, hb), lambda i, g: (g, i, 0)),
                  pl.BlockSpec((None, cb * c, hb), lambda i, g: (g, i, 0)),
                  pl.BlockSpec((None, cb, hb, 2 * c), lambda i, g: (g, i, 0, 0))],
        out_specs=[pl.BlockSpec((cb, c, wv), lambda i, g: (i, 0, g)),
                   pl.BlockSpec((cb, 2 * c, wv), lambda i, g: (i, 0, g)),
                   pl.BlockSpec((cb, c, wv), lambda i, g: (i, 0, g)),
                   pl.BlockSpec((cb, hb, c, c), lambda i, g: (i, g, 0, 0)),
                   pl.BlockSpec((cb, SUBLANES, wv), lambda i, g: (i, 0, g))],
        out_shape=[jax.ShapeDtypeStruct((nch, c, hv * HEAD), F32),
                   jax.ShapeDtypeStruct((nch, 2 * c, hv * HEAD), BF16),
                   jax.ShapeDtypeStruct((nch, c, hv * HEAD), BF16),
                   jax.ShapeDtypeStruct((nch, hv, c, c), BF16),
                   jax.ShapeDtypeStruct((nch, SUBLANES, hv * HEAD), F32)],
        compiler_params=_cp(("parallel", "parallel"), 40),
        name="gdn_prep",
    )(act, act, act, bcol, gcol, grow2)


def _gdn_scan_kernel(u_ref, wq_ref, kd_ref, a_ref, egl_ref, z_ref, nw_ref, o_ref, s_ref, *, hb, nc):
    c = CHUNK

    @pl.when(pl.program_id(2) == 0)
    def _():
        s_ref[...] = jnp.zeros_like(s_ref)

    nw = nw_ref[...]

    def body(n, carry):
        r0 = pl.multiple_of(n * c, c)
        heads = range(hb)
        lanes = [slice(hh * HEAD, (hh + 1) * HEAD) for hh in heads]
        s = [s_ref[hh] for hh in heads]
        res = [_dot(wq_ref[n, :, lanes[hh]], s[hh].astype(BF16)) for hh in heads]
        v16 = [(u_ref[n, :, lanes[hh]] - res[hh][:c]).astype(BF16) for hh in heads]
        upd = [_dot_tn(kd_ref[n, :, lanes[hh]], v16[hh]) for hh in heads]
        o_intra = [_dot(a_ref[n, hh], v16[hh]) for hh in heads]
        for hh in heads:
            s_ref[hh] = s[hh] * egl_ref[n, 0:1, lanes[hh]] + upd[hh]
        for hh in heads:
            o = res[hh][c:] + o_intra[hh]
            o_ref[pl.ds(r0, c), lanes[hh]] = _gated_out(o, z_ref[pl.ds(r0, c), lanes[hh]], nw).astype(o_ref.dtype)
        return carry

    lax.fori_loop(0, nc, body, 0)


def _gdn_scan(u3, wq3, kd3, a4, egl3, proj, norm_w, layer, b, t, hv, conv_dim, hb, nc):
    ntb = (t // CHUNK) // nc
    hg = hv // hb
    wv = hb * HEAD
    c = CHUNK
    kern = functools.partial(_gdn_scan_kernel, hb=hb, nc=nc)
    blk = lambda bi, g, tb: bi * ntb + tb
    return pl.pallas_call(
        kern,
        grid=(b, hg, ntb),
        in_specs=[pl.BlockSpec((nc, c, wv), lambda bi, g, tb: (blk(bi, g, tb), 0, g)),
                  pl.BlockSpec((nc, 2 * c, wv), lambda bi, g, tb: (blk(bi, g, tb), 0, g)),
                  pl.BlockSpec((nc, c, wv), lambda bi, g, tb: (blk(bi, g, tb), 0, g)),
                  pl.BlockSpec((nc, hb, c, c), lambda bi, g, tb: (blk(bi, g, tb), g, 0, 0)),
                  pl.BlockSpec((nc, SUBLANES, wv), lambda bi, g, tb: (blk(bi, g, tb), 0, g)),
                  pl.BlockSpec((nc * c, wv), lambda bi, g, tb: (blk(bi, g, tb), conv_dim // wv + g)),
                  pl.BlockSpec((None, 1, HEAD), lambda bi, g, tb: (layer, 0, 0))],
        out_specs=[pl.BlockSpec((nc * c, wv), lambda bi, g, tb: (blk(bi, g, tb), g)),
                   pl.BlockSpec((None, hb, HEAD, HEAD), lambda bi, g, tb: (bi, g, 0, 0))],
        out_shape=[jax.ShapeDtypeStruct((b * t, hv * HEAD), BF16),
                   jax.ShapeDtypeStruct((b, hv, HEAD, HEAD), F32)],
        compiler_params=_cp(("parallel", "parallel", "arbitrary"), 40),
        name="gdn_scan",
    )(u3, wq3, kd3, a4, egl3, proj, norm_w.reshape(-1, 1, HEAD))


def _gdn_step_kernel(q_ref, k_ref, v_ref, z_ref, bcol_ref, gcol_ref, g3_ref, nw_ref, s0_ref,
                     *rest, hb, bt, fill_layers):
    o_ref, s_ref = rest[-2], rest[-1]
    rows = lax.broadcasted_iota(jnp.int32, (bt, bt * HEAD), 0)
    cols = lax.broadcasted_iota(jnp.int32, (bt, bt * HEAD), 1)
    bd = (cols // HEAD) == rows
    bd2 = jnp.concatenate([bd, bd], axis=0)
    nw = nw_ref[...]
    beta_all = bcol_ref[...]
    g_all = gcol_ref[...]

    def block_diag(x, mask):
        return jnp.where(mask, jnp.concatenate([x] * bt, axis=1), 0.0).astype(BF16)

    for p in range(hb // 2):
        lanes_qk = slice(p * HEAD, (p + 1) * HEAD)
        qn = _l2norm(q_ref[:, lanes_qk]) * (HEAD ** -0.5)
        kn = _l2norm(k_ref[:, lanes_qk])
        qk = jnp.sum(qn * kn, axis=-1, keepdims=True)
        for r in range(2):
            hh = 2 * p + r
            lanes = slice(hh * HEAD, (hh + 1) * HEAD)
            beta = beta_all[:, hh:hh + 1]
            eg = jnp.exp(g_all[:, hh:hh + 1])
            u = v_ref[:, lanes] * beta
            w = kn * beta * eg
            s3 = s0_ref[:, hh]
            s2 = s3.reshape(bt * HEAD, HEAD).astype(BF16)
            res = _dot(block_diag(jnp.concatenate([w, qn * eg], axis=0), bd2), s2)
            v_new = u - res[:bt]
            o = res[bt:] + qk * v_new
            upd = _dot_tn(block_diag(kn, bd), v_new.astype(BF16))
            s_new = s3 * jnp.exp(g3_ref[hh]) + upd.reshape(bt, HEAD, HEAD)
            if fill_layers:
                for slot in range(fill_layers):
                    s_ref[slot, :, hh] = s_new
            else:
                s_ref[:, hh] = s_new
            o_ref[:, lanes] = _gated_out(o, z_ref[:, lanes], nw).astype(o_ref.dtype)


def _gdn_step(act, proj, bcol, gcol, g3, norm_w, state, layer, row0, bs, hv, key_dim, conv_dim,
              hb, bt, stacked=None):
    hg = hv // hb
    wqk = (hb // 2) * HEAD
    wv = hb * HEAD
    rb = row0 // bt
    n_layers = state.shape[0]
    first = stacked is None
    kern = functools.partial(_gdn_step_kernel, hb=hb, bt=bt, fill_layers=n_layers if first else 0)
    if first:
        state_out = pl.BlockSpec((n_layers, bt, hb, HEAD, HEAD), lambda i, g: (0, i, g, 0, 0))
        extra_in, extra_args, aliases = [], [], {}
    else:
        state_out = pl.BlockSpec((None, bt, hb, HEAD, HEAD), lambda i, g: (layer, i, g, 0, 0))
        extra_in, extra_args, aliases = [pl.BlockSpec(memory_space=pl.ANY)], [stacked], {9: 1}
    return pl.pallas_call(
        kern,
        grid=(bs // bt, hg),
        input_output_aliases=aliases,
        in_specs=[pl.BlockSpec((bt, wqk), lambda i, g: (i, g)),
                  pl.BlockSpec((bt, wqk), lambda i, g: (i, key_dim // wqk + g)),
                  pl.BlockSpec((bt, wv), lambda i, g: (i, 2 * key_dim // wv + g)),
                  pl.BlockSpec((bt, wv), lambda i, g: (rb + i, conv_dim // wv + g)),
                  pl.BlockSpec((None, bt, hb), lambda i, g: (g, i, 0)),
                  pl.BlockSpec((None, bt, hb), lambda i, g: (g, i, 0)),
                  pl.BlockSpec((None, hb, bt, 1, 1), lambda i, g: (g, 0, i, 0, 0)),
                  pl.BlockSpec((None, 1, HEAD), lambda i, g: (layer, 0, 0)),
                  pl.BlockSpec((None, bt, hb, HEAD, HEAD), lambda i, g: (layer, i, g, 0, 0))]
                 + extra_in,
        out_specs=[pl.BlockSpec((bt, wv), lambda i, g: (i, g)), state_out],
        out_shape=[jax.ShapeDtypeStruct((bs, hv * HEAD), BF16),
                   jax.ShapeDtypeStruct((n_layers, bs, hv, HEAD, HEAD), F32)],
        compiler_params=_cp(("parallel", "parallel"), 40),
        name="gdn_step",
    )(act, act, act, proj, bcol, gcol, g3, norm_w.reshape(-1, 1, HEAD), state, *extra_args)


def _by_head_group(x, hb):
    r, hv = x.shape
    return jnp.transpose(x.reshape(r, hv // hb, hb), (1, 0, 2))


def _gdn_mixer(x, h, p, j, dims, t, s_stacked):
    b, tlen, bs = dims["b"], dims["t"], dims["bs"]
    mp = b * tlen
    hv = p["gdn_a_log"].shape[1]
    conv_dim = p["gdn_conv_w"].shape[2]
    val_dim = hv * HEAD
    key_dim = (conv_dim - val_dim) // 2
    n_main = conv_dim + val_dim
    hb = t["hb"]
    wt_in = jnp.transpose(p["gdn_w_in"], (0, 2, 1))

    proj = _ws_matmul(h, wt_in, j, (0,), [], lambda accs, _: accs[0], n_main, F32, t["tm"],
                      t["tn_in"], "gdn_in_proj", transposed=True, n_sub=t["n_sub"])
    wt_b = wt_in[j, n_main:n_main + hv]
    wt_a = wt_in[j, n_main + hv:]
    beta, gc = _gdn_gates(h, wt_b, wt_a, p["gdn_a_log"], p["gdn_dt_bias"], j, mp, t["tm_gates"])

    act_p, gbuf_p = _gconv_prompt(proj, p["gdn_conv_w"], j, b, tlen, conv_dim, t["tc_gconv"])
    nch = mp // CHUNK
    bcol = _by_head_group(beta[:mp], hb)
    gcol = _by_head_group(gc[:mp], hb)
    grow = jnp.transpose(gcol.reshape(hv // hb, nch, CHUNK, hb), (0, 1, 3, 2))
    grow2 = jnp.concatenate([grow, grow], axis=-1)
    u3, wq3, kd3, a4, egl3 = _gdn_prep(act_p, bcol, gcol, grow2, b, tlen, hv, key_dim, hb, t["cb"])
    o_p, s_p = _gdn_scan(u3, wq3, kd3, a4, egl3, proj, p["gdn_norm_w"], j, b, tlen, hv, conv_dim,
                         min(t["hb_scan"], hv), min(t["nc_scan"], tlen // CHUNK))

    conv_buf_t = jnp.transpose(p["state_gdn_conv"], (0, 2, 1, 3))
    act_s, gbuf_s_t = _gconv_step(proj, conv_buf_t, p["gdn_conv_w"], j, mp, bs, conv_dim,
                                  t["tc_gconv"])
    bcol_s = _by_head_group(beta[mp:], hb)
    gcol_s = _by_head_group(gc[mp:], hb)
    g3 = jnp.transpose(gcol_s, (0, 2, 1)).reshape(hv // hb, hb, bs, 1, 1)
    o_s, s_s = _gdn_step(act_s, proj, bcol_s, gcol_s, g3, p["gdn_norm_w"],
                         p["state_gdn_recurrent"], j, mp, bs, hv, key_dim, conv_dim, hb, t["bt"],
                         s_stacked)

    o = jnp.concatenate([o_p, o_s], axis=0)
    d = x.shape[1]
    x = _ws_matmul(o, p["gdn_w_out"], j, (0,), [(x, _tile_spec(t["tm"], t["tn_out"]))],
                   lambda accs, e: e[0] + accs[0], d, F32, t["tm"], t["tn_out"], "gdn_out_proj",
                   n_sub=t["n_sub"])
    return x, s_p, s_s, gbuf_p, gbuf_s_t


def _layer_norm_silu(hc, lnw, lnb):
    mu = jnp.mean(hc, axis=-1, keepdims=True)
    xc = hc - mu
    y = xc * lax.rsqrt(jnp.mean(xc * xc, axis=-1, keepdims=True) + EPS)
    return _silu(y * lnw + lnb)


def _dwconv_prompt_kernel(cur_ref, prev_ref, w_ref, bdw_ref, lnw_ref, lnb_ref, o_ref, buf_ref,
                          win_ref, sh_ref, conv_ref, *, tt, kw, halo, rc, lc):
    jt = pl.program_id(1)
    d = cur_ref.shape[1]
    hist = kw - 1
    off = halo - hist
    nsh = sh_ref.shape[1]

    @pl.when(jt == 0)
    def _():
        win_ref[0:halo, :] = jnp.zeros((halo, d), F32)

    @pl.when(jt > 0)
    def _():
        win_ref[0:halo, :] = prev_ref[...]

    win_ref[halo:halo + tt, :] = cur_ref[...]
    for s in range(1, SUBLANES):
        sh_ref[s - 1] = win_ref[s:s + nsh, :]

    for r0 in range(0, tt, rc):
        for c0 in range(0, d, lc):
            lanes = slice(c0, c0 + lc)
            acc = cur_ref[r0:r0 + rc, lanes] * w_ref[hist:kw, lanes]
            for i in range(hist):
                o = off + i
                s = o % SUBLANES
                base = r0 + o - s
                if s == 0:
                    tap = win_ref[base:base + rc, lanes]
                else:
                    tap = sh_ref[s - 1, base:base + rc, lanes]
                acc = acc + tap * w_ref[i:i + 1, lanes]
            conv_ref[r0:r0 + rc, lanes] = acc + bdw_ref[:, lanes]
    o_ref[...] = _layer_norm_silu(conv_ref[...], lnw_ref[...], lnb_ref[...]).astype(o_ref.dtype)

    @pl.when(jt == pl.num_programs(1) - 1)
    def _():
        buf_ref[...] = cur_ref[tt - hist:tt, :]


def _dwconv_prompt(hc, w_dw, b_dw, ln_w, ln_b, layer, b, t, tt, halo=32):
    d = hc.shape[1]
    kw = w_dw.shape[1]
    ntb = t // tt
    hpb = tt // halo
    kern = functools.partial(_dwconv_prompt_kernel, tt=tt, kw=kw, halo=halo, rc=32, lc=512)
    vec = pl.BlockSpec((None, 1, d), lambda bi, jt: (layer, 0, 0))
    nsh = tt + halo - SUBLANES
    return pl.pallas_call(
        kern,
        grid=(b, ntb),
        in_specs=[pl.BlockSpec((tt, d), lambda bi, jt: (bi * ntb + jt, 0)),
                  pl.BlockSpec((halo, d), lambda bi, jt: (jnp.maximum((bi * ntb + jt) * hpb - 1, 0), 0)),
                  pl.BlockSpec((None, kw, d), lambda bi, jt: (layer, 0, 0)),
                  vec, vec, vec],
        out_specs=[pl.BlockSpec((tt, d), lambda bi, jt: (bi * ntb + jt, 0)),
                   pl.BlockSpec((None, kw - 1, d), lambda bi, jt: (bi, 0, 0))],
        out_shape=[jax.ShapeDtypeStruct((b * t, d), BF16),
                   jax.ShapeDtypeStruct((b, kw - 1, d), F32)],
        scratch_shapes=[pltpu.VMEM((tt + halo, d), F32),
                        pltpu.VMEM((SUBLANES - 1, nsh, d), F32),
                        pltpu.VMEM((tt, d), F32)],
        compiler_params=_cp(("parallel", "arbitrary"), 48),
        name="conf_dwconv_prompt",
    )(hc, hc, w_dw, b_dw.reshape(-1, 1, d), ln_w.reshape(-1, 1, d), ln_b.reshape(-1, 1, d))


def _dwconv_step_kernel(x_ref, buf_ref, w_ref, bdw_ref, lnw_ref, lnb_ref, o_ref, nbuf_ref, *, kw):
    x = x_ref[...]
    acc = x * w_ref[kw - 1:kw, :]
    for i in range(kw - 1):
        row = buf_ref[i]
        acc = acc + row * w_ref[i:i + 1, :]
        if i >= 1:
            nbuf_ref[i - 1] = row
    nbuf_ref[kw - 2] = x
    o_ref[...] = _layer_norm_silu(acc + bdw_ref[...], lnw_ref[...], lnb_ref[...]).astype(o_ref.dtype)


def _dwconv_step(hc, conv_buf_t, w_dw, b_dw, ln_w, ln_b, layer, row0, bs, bt):
    d = hc.shape[1]
    kw = w_dw.shape[1]
    rb = row0 // bt
    kern = functools.partial(_dwconv_step_kernel, kw=kw)
    vec = pl.BlockSpec((None, 1, d), lambda i: (layer, 0, 0))
    return pl.pallas_call(
        kern,
        grid=(bs // bt,),
        in_specs=[pl.BlockSpec((bt, d), lambda i: (rb + i, 0)),
                  pl.BlockSpec((None, kw - 1, bt, d), lambda i: (layer, 0, i, 0)),
                  pl.BlockSpec((None, kw, d), lambda i: (layer, 0, 0)),
                  vec, vec, vec],
        out_specs=[pl.BlockSpec((bt, d), lambda i: (i, 0)),
                   pl.BlockSpec((kw - 1, bt, d), lambda i: (0, i, 0))],
        out_shape=[jax.ShapeDtypeStruct((bs, d), BF16),
                   jax.ShapeDtypeStruct((kw - 1, bs, d), F32)],
        compiler_params=_cp(("parallel",), 40),
        name="conf_dwconv_step",
    )(hc, conv_buf_t, w_dw, b_dw.reshape(-1, 1, d), ln_w.reshape(-1, 1, d), ln_b.reshape(-1, 1, d))


def _glu_epilogue(accs, e):
    return (accs[0] + e[0]) * jax.nn.sigmoid(accs[1] + e[1])


def _conf_mixer(x, h, p, j, dims, t):
    b, tlen, bs = dims["b"], dims["t"], dims["bs"]
    mp = b * tlen
    d = x.shape[1]
    cd = p["conf_w_dw"].shape[2]
    tn = t["tn_pw1"]
    b1 = p["conf_b_pw1"].reshape(-1, 1, 2 * cd)
    hc = _ws_matmul(h, p["conf_w_pw1"], j, (0, cd // tn),
                    [(b1, _row_spec(tn, j)), (b1, _row_spec(tn, j, cd // tn))],
                    _glu_epilogue, cd, F32, t["tm_up"], tn, "conf_pw1", n_sub=t["n_sub"],
                    vmem_mib=56)
    y_p, cbuf_p = _dwconv_prompt(hc, p["conf_w_dw"], p["conf_b_dw"], p["conf_ln_w"], p["conf_ln_b"],
                                 j, b, tlen, t["tt_dw"])
    conv_buf_t = jnp.transpose(p["state_conformer_conv"], (0, 2, 1, 3))
    y_s, cbuf_s_t = _dwconv_step(hc, conv_buf_t, p["conf_w_dw"], p["conf_b_dw"],
                                 p["conf_ln_w"], p["conf_ln_b"], j, mp, bs, t["bt"])
    y = jnp.concatenate([y_p, y_s], axis=0)
    b2 = p["conf_b_pw2"].reshape(-1, 1, d)
    tn2 = t["tn_out"]
    x = _ws_matmul(y, p["conf_w_pw2"], j, (0,),
                   [(b2, _row_spec(tn2, j)), (x, _tile_spec(t["tm"], tn2))],
                   lambda accs, e: e[1] + (accs[0] + e[0]), d, F32, t["tm"], tn2, "conf_pw2",
                   n_sub=t["n_sub"])
    return x, cbuf_p, cbuf_s_t


def _tiles(m, d, f):
    if m == 8320 and d == 4096 and f == 11008:
        return dict(tm=1040, tm_up=1664, tm_norm=520, tn_up=256, tm_down=520, tn_down=512, tkw_down=1376,
                    tn_in=512, tn_out=512, tn_pw1=256, tc_gconv=256, tt_dw=128, tm_gates=640,
                    hb=4, bt=8, cb=4, hb_scan=8, nc_scan=8, n_sub=2)
    return dict(tm=m, tm_up=m, tm_norm=m, tn_up=256, tm_down=m, tn_down=256, tkw_down=256,
                tn_in=256, tn_out=256, tn_pw1=256, tc_gconv=256, tt_dw=64, tm_gates=m,
                hb=4, bt=8, cb=2, hb_scan=8, nc_scan=2, n_sub=2)


def _trunk(p, depth):
    xp, xs = p["x_prompt"], p["x_sample"]
    b, tlen, d = xp.shape
    bs = xs.shape[0]
    mp = b * tlen
    dims = dict(b=b, t=tlen, bs=bs)
    x = jnp.concatenate([xp.reshape(mp, d), xs.reshape(bs, d)], axis=0)
    f = p["ffn_pre_w_down"].shape[1]
    t = _tiles(x.shape[0], d, f)

    s_p, gb_p, gb_s, cb_p, cb_s = [], [], [], [], []
    s_s = None
    for i in range(depth):
        x = _ffn(x, p["ffn_norm_pre"], p["ffn_pre_w_gate_up"], p["ffn_pre_w_down"], i, t)
        h = _rms_norm(x, p["mixer_norm"], i, BF16, t["tm_norm"])
        j = i // 2
        if i % 2 == 0:
            x, sp, s_s, gp, gs = _gdn_mixer(x, h, p, j, dims, t, s_s)
            s_p.append(sp); gb_p.append(gp); gb_s.append(gs)
        else:
            x, cp, cs = _conf_mixer(x, h, p, j, dims, t)
            cb_p.append(cp); cb_s.append(cs)
        x = _ffn(x, p["ffn_norm_post"], p["ffn_post_w_gate_up"], p["ffn_post_w_down"], i, t)

    fn = p["final_norm"].reshape(1, d)
    tmf = 512 if mp % 512 == 0 else mp
    y_p = _rms_norm(x, fn, 0, F32, tmf, 0, mp)
    y_s = _rms_norm(x, fn, 0, F32, bs, mp, bs)
    gbuf_s = jnp.transpose(jnp.stack(gb_s), (0, 2, 1, 3))
    cbuf_s = jnp.transpose(jnp.stack(cb_s), (0, 2, 1, 3))
    return (y_p.reshape(b, tlen, d), y_s.reshape(bs, 1, d), jnp.stack(s_p), s_s,
            jnp.stack(gb_p), gbuf_s, jnp.stack(cb_p), cbuf_s)


def kernel(x_prompt, x_sample, state_gdn_recurrent, state_gdn_conv, state_conformer_conv,
           ffn_norm_pre, ffn_pre_w_gate_up, ffn_pre_w_down, mixer_norm, ffn_norm_post,
           ffn_post_w_gate_up, ffn_post_w_down, gdn_w_in, gdn_conv_w, gdn_a_log, gdn_dt_bias,
           gdn_norm_w, gdn_w_out, conf_w_pw1, conf_b_pw1, conf_w_dw, conf_b_dw, conf_ln_w,
           conf_ln_b, conf_w_pw2, conf_b_pw2, final_norm):
    p = dict(x_prompt=x_prompt, x_sample=x_sample, state_gdn_recurrent=state_gdn_recurrent,
             state_gdn_conv=state_gdn_conv, state_conformer_conv=state_conformer_conv,
             ffn_norm_pre=ffn_norm_pre, ffn_pre_w_gate_up=ffn_pre_w_gate_up,
             ffn_pre_w_down=ffn_pre_w_down, mixer_norm=mixer_norm, ffn_norm_post=ffn_norm_post,
             ffn_post_w_gate_up=ffn_post_w_gate_up, ffn_post_w_down=ffn_post_w_down,
             gdn_w_in=gdn_w_in, gdn_conv_w=gdn_conv_w, gdn_a_log=gdn_a_log,
             gdn_dt_bias=gdn_dt_bias, gdn_norm_w=gdn_norm_w, gdn_w_out=gdn_w_out,
             conf_w_pw1=conf_w_pw1, conf_b_pw1=conf_b_pw1, conf_w_dw=conf_w_dw,
             conf_b_dw=conf_b_dw, conf_ln_w=conf_ln_w, conf_ln_b=conf_ln_b,
             conf_w_pw2=conf_w_pw2, conf_b_pw2=conf_b_pw2, final_norm=final_norm)
    return _trunk(p, ffn_norm_pre.shape[0])
```
